```python
import jax
import jax.numpy as jnp
from jax import lax
import numpy as np

D_MODEL = 1024
BATCH = 1
SEQ = 16384
DEPTH = 2

D_MIX = 1024
HG_HEADS = 4
HG_KDIM = 128
HG_VDIM = 128
HG_WIDTH = HG_HEADS * HG_VDIM
HG_CHUNK = 64
SW_HEADS = 4
SW_HDIM = 64
SW_WIDTH = SW_HEADS * SW_HDIM
SW_PATTERNS = ((128, 1), (512, 4), (2048, 16))
SW_BLOCK = 128
RK_HEADS = 4
RK_HDIM = 64
RK_WIDTH = RK_HEADS * RK_HDIM
RK_DECAY_LORA = 64
RK_AAA_LORA = 64
RK_MV_LORA = 32
RK_GATE_LORA = 128
RK_GN_EPS = 64e-5
HG_IN = 4 * HG_WIDTH
SW_IN = 3 * SW_WIDTH
RK_IN = 3 * RK_WIDTH + RK_DECAY_LORA + RK_AAA_LORA + RK_GATE_LORA
N_IN = HG_IN + SW_IN + RK_IN
D_FF = 2816
CONV_W = 3
PLE_DIM = 256
NORM_EPS = 1e-6

kernel_name = 'hymba_style_hgrn2_dilated_rwkv7_hybrid'


def rmsnorm(x, g):
    xf = x.astype(jnp.float32)
    y = xf * lax.rsqrt(jnp.mean(xf * xf, axis=-1, keepdims=True) + NORM_EPS)
    return (y * g.astype(jnp.float32)).astype(x.dtype)


def hgrn2_mix(q_raw, f_raw, i_raw, g_raw, lower_bound, gnorm_g):
    f32 = jnp.float32
    B_, S_, _ = q_raw.shape
    nc = S_ // HG_CHUNK
    q = jax.nn.silu(q_raw.astype(f32))
    f = lower_bound + (1.0 - lower_bound) * jax.nn.sigmoid(f_raw.astype(f32))
    k = 1.0 - f
    log_f = jnp.log(f)

    def chunks(t):
        return t.reshape(B_, nc, HG_CHUNK, HG_HEADS, -1).transpose(1, 0, 3, 2, 4)

    causal = jnp.tril(jnp.ones((HG_CHUNK, HG_CHUNK), dtype=bool))

    def step(state, inp):
        qc, kc, vc, lfc = inp
        b = jnp.cumsum(lfc, axis=2)
        b_last = b[:, :, -1:, :]
        o_inter = jnp.einsum('bhck,bhkv->bhcv', qc * jnp.exp(b), state)
        rel = jnp.where(causal[:, :, None], b[:, :, :, None, :] - b[:, :, None, :, :], -jnp.inf)
        scores = jnp.einsum('bhik,bhjk,bhijk->bhij', qc, kc, jnp.exp(rel))
        o_intra = jnp.einsum('bhij,bhjv->bhiv', scores, vc)
        state = (jnp.exp(b_last[:, :, 0, :])[..., None] * state
                 + jnp.einsum('bhck,bhcv->bhkv', kc * jnp.exp(b_last - b), vc))
        return state, o_inter + o_intra

    s0 = jnp.zeros((B_, HG_HEADS, HG_KDIM, HG_VDIM), f32)
    _, o = lax.scan(step, s0, (chunks(q), chunks(k), chunks(i_raw.astype(f32)), chunks(log_f)))
    o = o.transpose(1, 0, 3, 2, 4).reshape(B_, S_, HG_HEADS, HG_VDIM)
    o = o * lax.rsqrt(jnp.mean(o * o, axis=-1, keepdims=True) + NORM_EPS)
    o = o.reshape(B_, S_, HG_WIDTH) * gnorm_g.astype(f32) * jax.nn.silu(g_raw.astype(f32))
    return o.astype(q_raw.dtype)


def banded_causal_attention(q, k, v, n_back):
    f32 = jnp.float32
    N, L, H, Dh = q.shape
    blk = SW_BLOCK
    nb = -(-L // blk)
    padw = ((0, 0), (0, nb * blk - L), (0, 0), (0, 0))
    qb = jnp.pad(q.astype(f32), padw).reshape(N, nb, blk, H, Dh)
    kb = jnp.pad(k.astype(f32), padw).reshape(N, nb, blk, H, Dh)
    vb = jnp.pad(v.astype(f32), padw).reshape(N, nb, blk, H, Dh)

    def with_prev(t):
        prev = jnp.pad(t, ((0, 0), (1, 0), (0, 0), (0, 0), (0, 0)))[:, :-1]
        return jnp.concatenate([prev, t], axis=2)

    kw, vw = with_prev(kb), with_prev(vb)
    s = jnp.einsum('nbqhd,nbkhd->nbhqk', qb, kw) * (Dh ** -0.5)
    qi = jnp.arange(blk)[:, None]
    kj = jnp.arange(2 * blk)[None, :]
    dist = qi - kj + blk
    kpos = jnp.arange(nb)[:, None, None] * blk + kj[None] - blk
    valid = (dist >= 0)[None] & (dist <= n_back)[None] & (kpos >= 0)
    s = jnp.where(valid[None, :, None], s, -jnp.inf)
    lse = jax.nn.logsumexp(s, axis=-1)
    prob = jnp.exp(s - lse[..., None])
    o = jnp.einsum('nbhqk,nbkhd->nbqhd', prob, vw).reshape(N, nb * blk, H, Dh)[:, :L]
    lse = lse.transpose(0, 1, 3, 2).reshape(N, nb * blk, H)[:, :L]
    return o, lse


def stride_gather(t, dil):
    B_, S_ = t.shape[:2]
    rest = t.shape[2:]
    return t.reshape((B_, S_ // dil, dil) + rest).swapaxes(1, 2).reshape((B_ * dil, S_ // dil) + rest)


def stride_scatter(t, B_, dil):
    L = t.shape[1]
    rest = t.shape[2:]
    return t.reshape((B_, dil, L) + rest).swapaxes(1, 2).reshape((B_, L * dil) + rest)


def dilated_window_mix(q_raw, k_raw, v_raw):
    B_, S_, _ = q_raw.shape
    q, k, v = (t.reshape(B_, S_, SW_HEADS, SW_HDIM) for t in (q_raw, k_raw, v_raw))
    outs, lses = [], []
    for window, dil in SW_PATTERNS:
        o, lse = banded_causal_attention(stride_gather(q, dil), stride_gather(k, dil),
                                         stride_gather(v, dil), window // dil)
        outs.append(stride_scatter(o, B_, dil))
        lses.append(stride_scatter(lse, B_, dil))
    wts = jax.nn.softmax(jnp.stack(lses, 0), axis=0)
    o = jnp.sum(wts[..., None] * jnp.stack(outs, 0), axis=0)
    return o.reshape(B_, S_, SW_WIDTH).astype(q_raw.dtype)


def rwkv7_mix(c, mu, w0, w2, a0, a2, g2, k_k, k_a, r_k, ln_w, ln_b, v_first, v_res):
    f32 = jnp.float32
    B_, S_, _ = c.shape
    cf = c.astype(f32)
    c_prev = jnp.pad(cf, ((0, 0), (1, 0), (0, 0)))[:, :-1]
    cm = cf + (c_prev - cf) * mu
    splits = [RK_WIDTH, 2 * RK_WIDTH, 3 * RK_WIDTH, 3 * RK_WIDTH + RK_DECAY_LORA,
              3 * RK_WIDTH + RK_DECAY_LORA + RK_AAA_LORA]
    r, k, v, wd, ad, gd = jnp.split(cm, splits, axis=-1)
    w = -jax.nn.softplus(-(w0 + jnp.tanh(wd) @ w2)) - 0.5
    decay = jnp.exp(-jnp.exp(w))
    a = jax.nn.sigmoid(a0 + ad @ a2)
    g = jax.nn.sigmoid(gd) @ g2
    if v_res is None:
        v_first = v
    else:
        v0, v1, v2 = v_res
        v = v + (v_first - v) * jax.nn.sigmoid(v0 + (v @ v1) @ v2)

    def heads(t):
        return t.reshape(B_, S_, RK_HEADS, RK_HDIM)

    kk = heads(k * k_k)
    kk = kk / jnp.maximum(jnp.linalg.norm(kk, axis=-1, keepdims=True), 1e-12)
    k = k * (1.0 + (a - 1.0) * k_a)
    rh, kh, vh, wh, ah = heads(r), heads(k), heads(v), heads(decay), heads(a)

    def step(state, inp):
        r_t, w_t, k_t, v_t, kk_t, a_t = inp
        sa = jnp.einsum('bhvk,bhk->bhv', state, -kk_t)
        state = (state * w_t[:, :, None, :] + sa[..., None] * (kk_t * a_t)[:, :, None, :]
                 + v_t[..., None] * k_t[:, :, None, :])
        return state, jnp.einsum('bhvk,bhk->bhv', state, r_t)

    s0 = jnp.zeros((B_, RK_HEADS, RK_HDIM, RK_HDIM), f32)
    xs = (rh.swapaxes(0, 1), wh.swapaxes(0, 1), kh.swapaxes(0, 1), vh.swapaxes(0, 1),
          kk.swapaxes(0, 1), ah.swapaxes(0, 1))
    _, y = lax.scan(step, s0, xs)
    y = y.swapaxes(0, 1)
    mean = jnp.mean(y, axis=-1, keepdims=True)
    var = jnp.mean((y - mean) ** 2, axis=-1, keepdims=True)
    y = ((y - mean) * lax.rsqrt(var + RK_GN_EPS)).reshape(B_, S_, RK_WIDTH) * ln_w + ln_b
    bonus = jnp.sum(rh * kh * r_k, axis=-1, keepdims=True) * vh
    y = (y + bonus.reshape(B_, S_, RK_WIDTH)) * g
    return y.astype(c.dtype), v_first


def conv_glu_ffn(hn, w_up, conv_w, conv_b, w_down):
    u = hn @ w_up
    S_ = u.shape[1]
    up = jnp.pad(u, ((0, 0), (CONV_W - 1, 0), (0, 0)))
    uc = conv_b + conv_w[0] * up[:, 0:S_]
    for j in range(1, CONV_W):
        uc = uc + conv_w[j] * up[:, j:j + S_]
    gate, val = jnp.split(uc, 2, axis=-1)
    return (jax.nn.silu(gate) * val) @ w_down


def setup_inputs(seed: int = 0) -> dict:
    key = jax.random.key(seed)
    ks = iter(jax.random.split(key, 40))

    def nrm(shape, scale):
        return jax.random.normal(next(ks), shape, jnp.float32) * scale

    def unif(shape, lo, hi):
        return jax.random.uniform(next(ks), shape, jnp.float32, lo, hi)

    L = DEPTH
    return {
        'x': nrm((BATCH, SEQ, D_MODEL), 1.0),
        'p': nrm((DEPTH, BATCH, SEQ, PLE_DIM), 1.0),
        'w_in': nrm((L, D_MODEL, N_IN), D_MODEL ** -0.5),
        'w_out': nrm((L, D_MIX, D_MODEL), D_MIX ** -0.5),
        'norm_mix_g': 1.0 + nrm((L, D_MODEL), 0.02),
        'norm_ffn_g': 1.0 + nrm((L, D_MODEL), 0.02),
        'norm_ple_g': 1.0 + nrm((L, D_MODEL), 0.02),
        'final_norm_g': 1.0 + nrm((D_MODEL,), 0.02),
        'hgrn_lower_bounds': nrm((L, HG_HEADS * HG_KDIM), 0.5),
        'hgrn_gnorm_g': 1.0 + nrm((L, HG_WIDTH), 0.02),
        'rwkv_mu': unif((L, RK_IN), 0.0, 1.0),
        'rwkv_w0': unif((L, RK_WIDTH), -5.0, -0.5),
        'rwkv_w2': nrm((L, RK_DECAY_LORA, RK_WIDTH), 0.1 * RK_DECAY_LORA ** -0.5),
        'rwkv_a0': nrm((L, RK_WIDTH), 0.1),
        'rwkv_a2': nrm((L, RK_AAA_LORA, RK_WIDTH), 0.1 * RK_AAA_LORA ** -0.5),
        'rwkv_g2': nrm((L, RK_GATE_LORA, RK_WIDTH), RK_GATE_LORA ** -0.5),
        'rwkv_k_k': 0.85 + nrm((L, RK_WIDTH), 0.05),
        'rwkv_k_a': 1.0 + nrm((L, RK_WIDTH), 0.05),
        'rwkv_r_k': nrm((L, RK_HEADS, RK_HDIM), 0.1),
        'rwkv_ln_w': 1.0 + nrm((L, RK_WIDTH), 0.02),
        'rwkv_ln_b': nrm((L, RK_WIDTH), 0.02),
        'rwkv_v0': 1.0 + nrm((L - 1, RK_WIDTH), 0.1),
        'rwkv_v1': nrm((L - 1, RK_WIDTH, RK_MV_LORA), 0.1 * RK_WIDTH ** -0.5),
        'rwkv_v2': nrm((L - 1, RK_MV_LORA, RK_WIDTH), 0.1 * RK_MV_LORA ** -0.5),
        'ffn_up': nrm((L, D_MODEL, 2 * D_FF), D_MODEL ** -0.5),
        'ffn_conv_w': nrm((L, CONV_W, 2 * D_FF), CONV_W ** -0.5),
        'ffn_conv_b': nrm((L, 2 * D_FF), 0.02),
        'ffn_down': nrm((L, D_FF, D_MODEL), D_FF ** -0.5),
        'ple_proj': nrm((L, PLE_DIM, D_MODEL), PLE_DIM ** -0.5),
        'ple_gate': nrm((L, D_MODEL, D_MODEL), D_MODEL ** -0.5),
    }


def reference(x, p, w_in, w_out, norm_mix_g, norm_ffn_g, norm_ple_g, final_norm_g,
              hgrn_lower_bounds, hgrn_gnorm_g, rwkv_mu, rwkv_w0, rwkv_w2, rwkv_a0, rwkv_a2,
              rwkv_g2, rwkv_k_k, rwkv_k_a, rwkv_r_k, rwkv_ln_w, rwkv_ln_b, rwkv_v0, rwkv_v1,
              rwkv_v2, ffn_up, ffn_conv_w, ffn_conv_b, ffn_down, ple_proj, ple_gate):
    lbs = jax.nn.softmax(hgrn_lower_bounds.astype(jnp.float32), axis=0)
    lbs = jnp.cumsum(lbs, axis=0) - lbs[0]
    splits = [HG_WIDTH, 2 * HG_WIDTH, 3 * HG_WIDTH, 4 * HG_WIDTH,
              HG_IN + SW_WIDTH, HG_IN + 2 * SW_WIDTH, HG_IN + 3 * SW_WIDTH]
    h = x
    v_first = None
    for l in range(DEPTH):
        z = rmsnorm(h, norm_mix_g[l]) @ w_in[l]
        a_q, a_f, a_i, a_g, b_q, b_k, b_v, c_in = jnp.split(z, splits, axis=-1)
        o_a = hgrn2_mix(a_q, a_f, a_i, a_g, lbs[l], hgrn_gnorm_g[l])
        o_b = dilated_window_mix(b_q, b_k, b_v)
        v_res = None if l == 0 else (rwkv_v0[l - 1], rwkv_v1[l - 1], rwkv_v2[l - 1])
        o_c, v_first = rwkv7_mix(c_in, rwkv_mu[l], rwkv_w0[l], rwkv_w2[l], rwkv_a0[l], rwkv_a2[l],
                                 rwkv_g2[l], rwkv_k_k[l], rwkv_k_a[l], rwkv_r_k[l], rwkv_ln_w[l],
                                 rwkv_ln_b[l], v_first, v_res)
        h = h + jnp.concatenate([o_a, o_b.astype(o_a.dtype), o_c.astype(o_a.dtype)], axis=-1) @ w_out[l]
        h = h + conv_glu_ffn(rmsnorm(h, norm_ffn_g[l]), ffn_up[l], ffn_conv_w[l], ffn_conv_b[l], ffn_down[l])
        gate = jax.nn.sigmoid(rmsnorm(h, norm_ple_g[l]) @ ple_gate[l])
        h = h + gate * (p[l] @ ple_proj[l])
    return rmsnorm(h, final_norm_g)
```

```python
import functools

import numpy as np
import jax
import jax.numpy as jnp
from jax import lax
from jax.experimental import pallas as pl
from jax.experimental.pallas import tpu as pltpu

F32 = jnp.float32
BF16 = jnp.bfloat16

D_MODEL = 1024
SEQ = 16384
DEPTH = 2
HG_HEADS = 4
HG_DIM = 128
HG_WIDTH = 512
SW_HEADS = 4
SW_HDIM = 64
SW_WIDTH = 256
SW_PATTERNS = ((128, 1), (512, 4), (2048, 16))
SW_BLOCK = 128
RK_HEADS = 4
RK_HDIM = 64
RK_WIDTH = 256
RK_GN_EPS = 64e-5
N_IN = 3840
D_FF = 2816
PLE_DIM = 256
NORM_EPS = 1e-6

LANES = 128
MXU_DIM = 256
VMEM_LIMIT_BYTES = 56 * 1024 * 1024

ROW_TILE = 512
FFN_COLS = 256
HG_TILE = 256
HG_BLK = 16
AT_TILE = 512
RK_CHUNK = 64
RK_PASSES = 3

NN = (((1,), (0,)), ((), ()))
NT = (((1,), (1,)), ((), ()))


def _dot(a, b, dims=NN):
    return lax.dot_general(a, b, dims, preferred_element_type=F32)


def _split2(x):
    hi = x.astype(BF16)
    lo = (x - hi.astype(F32)).astype(BF16)
    return hi, lo


def _mm(a, b, passes=1, dims=NN):
    if passes == 1:
        return _dot(a.astype(BF16), b.astype(BF16), dims)
    ah, al = _split2(a)
    bh, bl = _split2(b)
    return _dot(ah, bh, dims) + (_dot(al, bh, dims) + _dot(ah, bl, dims))


def _split3(x):
    x1 = x.astype(BF16)
    r1 = x - x1.astype(F32)
    x2 = r1.astype(BF16)
    x3 = (r1 - x2.astype(F32)).astype(BF16)
    return x1, x2, x3


def _sel_left(m01, x):
    x1, x2, x3 = _split3(x)
    return _dot(m01, x1) + (_dot(m01, x2) + _dot(m01, x3))


def _sel_right(x, m01):
    x1, x2, x3 = _split3(x)
    return _dot(x1, m01) + (_dot(x2, m01) + _dot(x3, m01))


def _sigmoid(x):
    return 1.0 / (1.0 + jnp.exp(-x))


def _silu(x):
    return x * _sigmoid(x)


def _rms(x, g):
    return x * lax.rsqrt(jnp.mean(x * x, axis=-1, keepdims=True) + NORM_EPS) * g


def _vmem_spec():
    return pl.BlockSpec(memory_space=pltpu.VMEM)


def _params(sem):
    return pltpu.CompilerParams(dimension_semantics=sem, vmem_limit_bytes=VMEM_LIMIT_BYTES)


def _inproj_kernel(h_ref, g_ref, w_ref, z_ref):
    y = _rms(h_ref[...], g_ref[...])
    z_ref[...] = _dot(y.astype(BF16), w_ref[...])


def _inproj(h, g, w_bf):
    return pl.pallas_call(
        _inproj_kernel,
        grid=(SEQ // ROW_TILE,),
        in_specs=[pl.BlockSpec((ROW_TILE, D_MODEL), lambda i: (i, 0)),
                  _vmem_spec(), _vmem_spec()],
        out_specs=pl.BlockSpec((ROW_TILE, N_IN), lambda i: (i, 0)),
        out_shape=jax.ShapeDtypeStruct((SEQ, N_IN), F32),
        compiler_params=_params(("arbitrary",)),
        name="inproj",
    )(h, g, w_bf)


def _hgrn_kernel(q_ref, f_ref, i_ref, g_ref, lbp_ref, gn_ref, bt_ref, bo_ref, o_ref,
                 st_s, qe_s, ke_s, b_s, qq_s, kk_s, dec_s, vt_s, *, layer):
    @pl.when(pl.program_id(1) == 0)
    def _():
        st_s[...] = jnp.zeros_like(st_s)

    lbp = lbp_ref[...]
    e = jnp.exp(lbp - jnp.max(lbp, axis=0, keepdims=True))
    sm = e / jnp.sum(e, axis=0, keepdims=True)
    lb = jnp.zeros((1, HG_DIM), F32)
    for j in range(1, layer + 1):
        lb = lb + sm[j:j + 1, :]

    q = _silu(q_ref[...])
    f = lb + (1.0 - lb) * _sigmoid(f_ref[...])
    kk = 1.0 - f
    lf = jnp.log(f)
    b = _sel_left(bt_ref[...], lf)
    tot = _sel_left(bo_ref[...], lf)
    qe_s[...] = q * jnp.exp(b)
    ke_s[...] = kk * jnp.exp(tot - b)
    b_s[...] = b
    qq_s[...] = q
    kk_s[...] = kk
    dec_s[...] = jnp.exp(tot)
    vt_s[...] = i_ref[...].T

    rid = lax.broadcasted_iota(jnp.int32, (HG_BLK, HG_DIM), 0)
    trow = lax.broadcasted_iota(jnp.int32, (HG_TILE, HG_DIM), 0)

    def body(n, carry):
        r0 = pl.multiple_of(n * HG_BLK, HG_BLK)
        rows = pl.ds(r0, HG_BLK)
        st = st_s[...]
        o = _dot(qe_s[rows, :].astype(BF16), st.astype(BF16), NT)
        b_n = b_s[rows, :]
        q_n = qq_s[rows, :]
        k_n = kk_s[rows, :]
        v_n = i_ref[rows, :]
        od = jnp.zeros((HG_BLK, HG_DIM), F32)
        for i in range(HG_BLK):
            wgt = jnp.where(rid <= i, jnp.exp(b_n[i:i + 1, :] - b_n) * (q_n[i:i + 1, :] * k_n), 0.0)
            s = jnp.sum(wgt, axis=-1, keepdims=True)
            oi = jnp.sum(s * v_n, axis=0, keepdims=True)
            od = jnp.where(rid == i, oi, od)
        o_ref[rows, :] = o + od
        kem = jnp.where((trow >= r0) & (trow < r0 + HG_BLK), ke_s[...], 0.0)
        upd = _dot(vt_s[...].astype(BF16), kem.astype(BF16))
        st_s[...] = st * dec_s[pl.ds(r0, 1), :] + upd
        return carry

    lax.fori_loop(0, HG_TILE // HG_BLK, body, 0)

    o = o_ref[...]
    o = o * lax.rsqrt(jnp.mean(o * o, axis=-1, keepdims=True) + NORM_EPS)
    o_ref[...] = o * gn_ref[...] * _silu(g_ref[...])


def _hgrn(z, lb_params, gnorm_g, layer):
    t = np.arange(HG_TILE)
    same = (t[:, None] // HG_BLK) == (t[None, :] // HG_BLK)
    bt = jnp.asarray(same & (t[None, :] <= t[:, None]), BF16)
    bo = jnp.asarray(same, BF16)
    col = lambda off: pl.BlockSpec((HG_TILE, HG_DIM), lambda h, i: (i, off + h))
    scr = lambda shape: pltpu.VMEM(shape, F32)
    return pl.pallas_call(
        functools.partial(_hgrn_kernel, layer=layer),
        grid=(HG_HEADS, SEQ // HG_TILE),
        in_specs=[col(0), col(HG_HEADS), col(2 * HG_HEADS), col(3 * HG_HEADS),
                  pl.BlockSpec((DEPTH, HG_DIM), lambda h, i: (0, h)),
                  pl.BlockSpec((1, HG_DIM), lambda h, i: (0, h)),
                  _vmem_spec(), _vmem_spec()],
        out_specs=pl.BlockSpec((HG_TILE, HG_DIM), lambda h, i: (i, h)),
        out_shape=jax.ShapeDtypeStruct((SEQ, HG_WIDTH), F32),
        scratch_shapes=[scr((HG_DIM, HG_DIM))] + [scr((HG_TILE, HG_DIM))] * 6 + [scr((HG_DIM, HG_TILE))],
        compiler_params=_params(("arbitrary", "arbitrary")),
        name="hgrn2",
    )(z, z, z, z, lb_params, gnorm_g, bt, bo)


def _attn_kernel(*refs, first, last):
    q_ref, ko_ref, vo_ref, kp_ref, vp_ref = refs[:5]
    refs = refs[5:]
    if not first:
        acc_in, m_in, l_in = refs[:3]
        refs = refs[3:]
    outs = refs
    i = pl.program_id(1)
    shape = (SW_BLOCK, 2 * SW_BLOCK)
    lane_head = lax.broadcasted_iota(jnp.int32, (SW_BLOCK, SW_WIDTH), 1) >> 6
    qi = lax.broadcasted_iota(jnp.int32, shape, 0)
    kj = lax.broadcasted_iota(jnp.int32, shape, 1)
    band = (kj >= qi) & (kj <= qi + SW_BLOCK)
    for j in range(AT_TILE // SW_BLOCK):
        rows = slice(j * SW_BLOCK, (j + 1) * SW_BLOCK)
        q = q_ref[rows, :] * (SW_HDIM ** -0.5)
        if j == 0:
            kw = jnp.concatenate([kp_ref[...], ko_ref[0:SW_BLOCK, :]], axis=0)
            vw = jnp.concatenate([vp_ref[...], vo_ref[0:SW_BLOCK, :]], axis=0)
            valid = band & ((kj >= SW_BLOCK) | (i > 0))
        else:
            win = slice((j - 1) * SW_BLOCK, (j + 1) * SW_BLOCK)
            kw = ko_ref[win, :]
            vw = vo_ref[win, :]
            valid = band
        kwb = kw.astype(BF16)
        vwb = vw.astype(BF16)
        acc = jnp.zeros((SW_BLOCK, SW_WIDTH), F32)
        mf = jnp.zeros((SW_BLOCK, SW_WIDTH), F32)
        lf = jnp.zeros((SW_BLOCK, SW_WIDTH), F32)
        for h in range(SW_HEADS):
            mh = lane_head == h
            s = _dot(jnp.where(mh, q, 0.0).astype(BF16), kwb, NT)
            s = jnp.where(valid, s, -1e30)
            m = jnp.max(s, axis=-1, keepdims=True)
            pe = jnp.exp(s - m)
            l = jnp.sum(pe, axis=-1, keepdims=True)
            pv = _dot(pe.astype(BF16), vwb)
            acc = jnp.where(mh, pv, acc)
            mf = jnp.where(mh, m, mf)
            lf = jnp.where(mh, l, lf)
        if not first:
            m_old = m_in[rows, :]
            m_new = jnp.maximum(m_old, mf)
            a_old = jnp.exp(m_old - m_new)
            a_cur = jnp.exp(mf - m_new)
            acc = acc_in[rows, :] * a_old + acc * a_cur
            lf = l_in[rows, :] * a_old + lf * a_cur
            mf = m_new
        if last:
            outs[0][rows, :] = acc / lf
        else:
            outs[0][rows, :] = acc
            outs[1][rows, :] = mf
            outs[2][rows, :] = lf


def _attn_pattern(z, dil, state, last):
    first = state is None
    length = SEQ // dil
    zv = z.reshape(length, dil * N_IN)
    cpb = N_IN // SW_WIDTH
    qcol = (HG_HEADS * 4 * HG_DIM) // SW_WIDTH
    sub = AT_TILE // SW_BLOCK
    own = lambda off: pl.BlockSpec((AT_TILE, SW_WIDTH), lambda r, i: (i, r * cpb + qcol + off))
    prev = lambda off: pl.BlockSpec((SW_BLOCK, SW_WIDTH),
                                    lambda r, i: (jnp.maximum(i * sub - 1, 0), r * cpb + qcol + off))
    st_spec = pl.BlockSpec((AT_TILE, SW_WIDTH), lambda r, i: (i, r))
    in_specs = [own(0), own(1), own(2), prev(1), prev(2)]
    args = [zv, zv, zv, zv, zv]
    if not first:
        in_specs += [st_spec] * 3
        args += [a.reshape(length, dil * SW_WIDTH) for a in state]
    n_out = 1 if last else 3
    sds = jax.ShapeDtypeStruct((length, dil * SW_WIDTH), F32)
    outs = pl.pallas_call(
        functools.partial(_attn_kernel, first=first, last=last),
        grid=(dil, length // AT_TILE),
        in_specs=in_specs,
        out_specs=[st_spec] * n_out,
        out_shape=[sds] * n_out,
        compiler_params=_params(("arbitrary", "arbitrary")),
        name=f"dilated_attn_d{dil}",
    )(*args)
    return [o.reshape(SEQ, SW_WIDTH) for o in outs]


def _dilated_mix(z):
    state = None
    for idx, (window, dil) in enumerate(SW_PATTERNS):
        assert window // dil == SW_BLOCK
        state = _attn_pattern(z, dil, state, last=idx == len(SW_PATTERNS) - 1)
    return state[0]


def _rwkv_kernel(*refs, has_vres):
    (zr_ref, zk_ref, zv_ref, zl_ref, mu_ref, w0_ref, a0_ref, w2_ref, a2_ref, g2_ref,
     kkw_ref, kaw_ref, rkw_ref, lnw_ref, lnb_ref, bd_ref, tri_ref) = refs[:17]
    refs = refs[17:]
    if has_vres:
        vf_ref, v0_ref, v1_ref, v2_ref = refs[:4]
        refs = refs[4:]
        y_ref, st_s, cb_s = refs
    else:
        y_ref, vfo_ref, st_s, cb_s = refs
    C = RK_CHUNK

    @pl.when(pl.program_id(0) == 0)
    def _():
        st_s[...] = jnp.zeros_like(st_s)
        cb_s[...] = jnp.zeros_like(cb_s)

    def shift_mix(idx, ref):
        c = ref[...]
        cb_s[idx, 8:8 + C, :] = c
        cp = cb_s[idx, 7:7 + C, :]
        cb_s[idx, 0:8, :] = c[C - 8:C, :]
        return c + (cp - c) * mu_ref[idx:idx + 1, :]

    r = shift_mix(0, zr_ref)
    k0 = shift_mix(1, zk_ref)
    v = shift_mix(2, zv_ref)
    lo = shift_mix(3, zl_ref)

    bd = bd_ref[...]
    wpre = w0_ref[...] + _mm(jnp.tanh(lo), w2_ref[...])
    sp = jnp.maximum(-wpre, 0.0) + jnp.log(1.0 + jnp.exp(-jnp.abs(wpre)))
    logw = -jnp.exp(-sp - 0.5)
    a = _sigmoid(a0_ref[...] + _mm(lo, a2_ref[...]))
    g = _mm(_sigmoid(lo), g2_ref[...])
    if has_vres:
        mix = _sigmoid(v0_ref[...] + _mm(_mm(v, v1_ref[...]), v2_ref[...]))
        v = v + (vf_ref[...] - v) * mix
    else:
        vfo_ref[...] = v
    kk = k0 * kkw_ref[...]
    kk = kk / jnp.maximum(jnp.sqrt(_sel_right(kk * kk, bd)), 1e-12)
    k = k0 * (1.0 + (a - 1.0) * kaw_ref[...])
    beta = kk * a

    lw = _sel_left(tri_ref[...], logw)
    lwc = lw[C - 1:C, :]
    e_neg = jnp.exp(-lw)
    e_end = jnp.exp(lwc - lw)

    ri = lax.broadcasted_iota(jnp.int32, (MXU_DIM, MXU_DIM), 0)
    ci = lax.broadcasted_iota(jnp.int32, (MXU_DIM, MXU_DIM), 1)
    same = (ri >> 6) == (ci >> 6)
    bdf = jnp.where(same, 1.0, 0.0)

    def stack(x):
        return jnp.concatenate([x] * RK_HEADS, axis=0) * bdf

    kap4 = stack(kk * jnp.exp(lw - logw))
    bh4 = stack(beta * e_neg)
    kh4 = stack(k * e_neg)
    rh4 = stack(r * jnp.exp(lw))
    bw4 = stack(beta * e_end)
    kw4 = stack(k * e_end)
    v4 = stack(v)

    mm = functools.partial(_mm, passes=RK_PASSES)
    strict = same & ((ci & 63) < (ri & 63))
    lower = same & ((ci & 63) <= (ri & 63))
    lmat = jnp.where(strict, mm(kap4, bh4, dims=NT), 0.0)
    a_k = jnp.where(strict, mm(kap4, kh4, dims=NT), 0.0)
    a_rk = jnp.where(lower, mm(rh4, kh4, dims=NT), 0.0)
    a_rb = jnp.where(lower, mm(rh4, bh4, dims=NT), 0.0)

    x = None
    for s1 in range(6):
        msk = ((ri >> (s1 + 1)) == (ci >> (s1 + 1))) & (((ri >> s1) & 1) == 1) & (((ci >> s1) & 1) == 0)
        lm = jnp.where(msk, lmat, 0.0)
        if x is None:
            x = jnp.where(ri == ci, 1.0, 0.0) - lm
        else:
            x = x - mm(mm(x, lm), x)

    m1 = mm(x, kap4)
    m2 = mm(x, mm(a_k, v4))
    q1 = rh4 - mm(a_rb, m1)
    q2 = mm(a_rk, v4) - mm(a_rb, m2)
    bw4t = bw4.T
    pt = jnp.where(ri == ci, jnp.exp(lwc), 0.0) - mm(bw4t, m1)
    rt = mm(kw4.T, v4) - mm(bw4t, m2)

    st = st_s[...]
    y4 = mm(q1, st) + q2
    st_s[...] = mm(pt, st) + rt
    y = y4[0:C, :]
    for h in range(1, RK_HEADS):
        y = y + y4[h * C:(h + 1) * C, :]

    inv_n = 1.0 / RK_HDIM
    mean = _sel_right(y, bd) * inv_n
    d = y - mean
    var = _sel_right(d * d, bd) * inv_n
    yn = d * lax.rsqrt(var + RK_GN_EPS) * lnw_ref[...] + lnb_ref[...]
    bonus = _sel_right(r * k * rkw_ref[...], bd) * v
    y_ref[...] = (yn + bonus) * g


def _rwkv(z, prm, v_first):
    has_vres = v_first is not None
    C = RK_CHUNK
    ccol = (HG_HEADS * 4 * HG_DIM + 3 * SW_WIDTH) // RK_WIDTH
    zcol = lambda off: pl.BlockSpec((C, RK_WIDTH), lambda i: (i, ccol + off))
    row = pl.BlockSpec((C, RK_WIDTH), lambda i: (i, 0))
    lane = np.arange(RK_WIDTH)
    bd = jnp.asarray((lane[:, None] // RK_HDIM) == (lane[None, :] // RK_HDIM), BF16)
    t = np.arange(C)
    tri = jnp.asarray(t[None, :] <= t[:, None], BF16)
    args = [z, z, z, z, prm["mu"], prm["w0"], prm["a0"], prm["w2"], prm["a2"], prm["g2"],
            prm["k_k"], prm["k_a"], prm["r_k"], prm["ln_w"], prm["ln_b"], bd, tri]
    in_specs = [zcol(0), zcol(1), zcol(2), zcol(3)] + [_vmem_spec()] * 13
    if has_vres:
        args += [v_first, prm["v0"], prm["v1"], prm["v2"]]
        in_specs += [row] + [_vmem_spec()] * 3
        out_specs = row
        out_shape = jax.ShapeDtypeStruct((SEQ, RK_WIDTH), F32)
    else:
        out_specs = [row, row]
        out_shape = [jax.ShapeDtypeStruct((SEQ, RK_WIDTH), F32)] * 2
    out = pl.pallas_call(
        functools.partial(_rwkv_kernel, has_vres=has_vres),
        grid=(SEQ // C,),
        in_specs=in_specs,
        out_specs=out_specs,
        out_shape=out_shape,
        scratch_shapes=[pltpu.VMEM((MXU_DIM, MXU_DIM), F32), pltpu.VMEM((4, C + 8, RK_WIDTH), F32)],
        compiler_params=_params(("arbitrary",)),
        name="rwkv7",
    )(*args)
    if has_vres:
        return out, v_first
    return out[0], out[1]


def _post_kernel(h_ref, oa_ref, ob_ref, oc_ref, p_ref, wout_ref, gffn_ref, wup_ref, cw_ref, cb_ref,
                 wdn_ref, gple_ref, wg_ref, wp_ref, gfin_ref, out_ref,
                 hn_s, acc_s, carry_s, ubuf_s, *, final):
    T = ROW_TILE

    @pl.when(pl.program_id(0) == 0)
    def _():
        carry_s[...] = jnp.zeros_like(carry_s)

    mix = (_dot(oa_ref[...].astype(BF16), wout_ref[0:HG_WIDTH, :])
           + _dot(ob_ref[...].astype(BF16), wout_ref[HG_WIDTH:HG_WIDTH + SW_WIDTH, :])
           + _dot(oc_ref[...].astype(BF16), wout_ref[HG_WIDTH + SW_WIDTH:, :]))
    h1 = h_ref[...] + mix
    hn_s[...] = _rms(h1, gffn_ref[...]).astype(BF16)
    acc_s[...] = h1

    def conv_cols(slot, col):
        cols = pl.ds(col, FFN_COLS)
        u = _dot(hn_s[...], wup_ref[:, cols])
        ubuf_s[slot, 0:8, :] = carry_s[:, cols]
        ubuf_s[slot, 8:8 + T, :] = u
        carry_s[:, cols] = u[T - 8:T, :]
        return (cb_ref[:, cols] + cw_ref[0:1, cols] * ubuf_s[slot, 6:6 + T, :]
                + cw_ref[1:2, cols] * ubuf_s[slot, 7:7 + T, :] + cw_ref[2:3, cols] * u)

    def body(j, carry):
        c0 = pl.multiple_of(j * FFN_COLS, FFN_COLS)
        gate = conv_cols(0, c0)
        val = conv_cols(1, pl.multiple_of(c0 + D_FF, LANES))
        act = (_silu(gate) * val).astype(BF16)
        acc_s[...] += _dot(act, wdn_ref[pl.ds(c0, FFN_COLS), :])
        return carry

    lax.fori_loop(0, D_FF // FFN_COLS, body, 0)

    h2 = acc_s[...]
    gate = _sigmoid(_dot(_rms(h2, gple_ref[...]).astype(BF16), wg_ref[...]))
    pe = _dot(p_ref[...].astype(BF16), wp_ref[...])
    h3 = h2 + gate * pe
    out_ref[...] = _rms(h3, gfin_ref[...]) if final else h3


def _post(h, o_a, o_b, o_c, p_l, prm, final):
    T = ROW_TILE
    tile = lambda w: pl.BlockSpec((T, w), lambda i: (i, 0))
    return pl.pallas_call(
        functools.partial(_post_kernel, final=final),
        grid=(SEQ // T,),
        in_specs=[tile(D_MODEL), tile(HG_WIDTH), tile(SW_WIDTH), tile(RK_WIDTH), tile(PLE_DIM)]
                 + [_vmem_spec()] * 10,
        out_specs=tile(D_MODEL),
        out_shape=jax.ShapeDtypeStruct((SEQ, D_MODEL), F32),
        scratch_shapes=[pltpu.VMEM((T, D_MODEL), BF16), pltpu.VMEM((T, D_MODEL), F32),
                        pltpu.VMEM((8, 2 * D_FF), F32), pltpu.VMEM((2, T + 8, FFN_COLS), F32)],
        compiler_params=_params(("arbitrary",)),
        name="outproj_ffn_ple",
    )(h, o_a, o_b, o_c, p_l, prm["w_out"], prm["g_ffn"], prm["w_up"], prm["conv_w"], prm["conv_b"],
      prm["w_down"], prm["g_ple"], prm["w_gate"], prm["w_ple"], prm["g_final"])


def _pad_rows(w, start, total):
    out = jnp.zeros((total, w.shape[1]), w.dtype)
    return lax.dynamic_update_slice(out, w, (start, 0))


def kernel(x, p, w_in, w_out, norm_mix_g, norm_ffn_g, norm_ple_g, final_norm_g, hgrn_lower_bounds, hgrn_gnorm_g, rwkv_mu, rwkv_w0, rwkv_w2, rwkv_a0, rwkv_a2, rwkv_g2, rwkv_k_k, rwkv_k_a, rwkv_r_k, rwkv_ln_w, rwkv_ln_b, rwkv_v0, rwkv_v1, rwkv_v2, ffn_up, ffn_conv_w, ffn_conv_b, ffn_down, ple_proj, ple_gate):
    row = lambda a: a.reshape(1, -1)
    h = x.reshape(SEQ, D_MODEL)
    v_first = None
    for l in range(DEPTH):
        z = _inproj(h, row(norm_mix_g[l]), w_in[l].astype(BF16))
        o_a = _hgrn(z, hgrn_lower_bounds, row(hgrn_gnorm_g[l]), l)
        o_b = _dilated_mix(z)
        rk = {
            "mu": rwkv_mu[l].reshape(4, RK_WIDTH),
            "w0": row(rwkv_w0[l]), "a0": row(rwkv_a0[l]),
            "w2": _pad_rows(rwkv_w2[l], 0, RK_WIDTH).astype(BF16),
            "a2": _pad_rows(rwkv_a2[l], 64, RK_WIDTH).astype(BF16),
            "g2": _pad_rows(rwkv_g2[l], 128, RK_WIDTH).astype(BF16),
            "k_k": row(rwkv_k_k[l]), "k_a": row(rwkv_k_a[l]), "r_k": row(rwkv_r_k[l]),
            "ln_w": row(rwkv_ln_w[l]), "ln_b": row(rwkv_ln_b[l]),
        }
        if l > 0:
            rk["v0"] = row(rwkv_v0[l - 1])
            rk["v1"] = jnp.pad(rwkv_v1[l - 1], ((0, 0), (0, LANES - rwkv_v1.shape[-1]))).astype(BF16)
            rk["v2"] = _pad_rows(rwkv_v2[l - 1], 0, LANES).astype(BF16)
        o_c, v_first = _rwkv(z, rk, v_first)
        post = {
            "w_out": w_out[l].astype(BF16), "g_ffn": row(norm_ffn_g[l]),
            "w_up": ffn_up[l].astype(BF16), "conv_w": ffn_conv_w[l], "conv_b": row(ffn_conv_b[l]),
            "w_down": ffn_down[l].astype(BF16), "g_ple": row(norm_ple_g[l]),
            "w_gate": ple_gate[l].astype(BF16), "w_ple": ple_proj[l].astype(BF16),
            "g_final": row(final_norm_g),
        }
        h = _post(h, o_a, o_b, o_c, p[l].reshape(SEQ, PLE_DIM), post, final=l == DEPTH - 1)
    return h.reshape(1, SEQ, D_MODEL)
```

```python
import functools

import numpy as np
import jax
import jax.numpy as jnp
from jax import lax
from jax.experimental import pallas as pl
from jax.experimental.pallas import tpu as pltpu

F32 = jnp.float32
BF16 = jnp.bfloat16

D_MODEL = 1024
SEQ = 16384
DEPTH = 2
HG_HEADS = 4
HG_DIM = 128
HG_WIDTH = 512
SW_HEADS = 4
SW_HDIM = 64
SW_WIDTH = 256
SW_PATTERNS = ((128, 1), (512, 4), (2048, 16))
SW_BLOCK = 128
RK_HEADS = 4
RK_HDIM = 64
RK_WIDTH = 256
RK_GN_EPS = 64e-5
N_IN = 3840
D_FF = 2816
PLE_DIM = 256
NORM_EPS = 1e-6

LANES = 128
MXU_DIM = 256
VMEM_LIMIT_BYTES = 56 * 1024 * 1024

ROW_TILE = 512
FFN_COLS = 256
HG_TILE = 256
HG_BLK = 16
AT_SPAN = SW_BLOCK * max(d for _, d in SW_PATTERNS)
ATTN_MASKED = -1e30
RK_CHUNK = 64
RK_TILE = 128
RK_LEVELS = 6
RK_PASS = {"gram": 1, "inv": 1, "apply": 1, "state": 1}

NN = (((1,), (0,)), ((), ()))
NT = (((1,), (1,)), ((), ()))


def _dot(a, b, dims=NN):
    return lax.dot_general(a, b, dims, preferred_element_type=F32)


def _split2(x):
    hi = x.astype(BF16)
    lo = (x - hi.astype(F32)).astype(BF16)
    return hi, lo


def _mm(a, b, passes=1, dims=NN):
    if passes == 1:
        return _dot(a.astype(BF16), b.astype(BF16), dims)
    ah, al = _split2(a)
    bh, bl = _split2(b)
    return _dot(ah, bh, dims) + (_dot(al, bh, dims) + _dot(ah, bl, dims))


def _split3(x):
    x1 = x.astype(BF16)
    r1 = x - x1.astype(F32)
    x2 = r1.astype(BF16)
    x3 = (r1 - x2.astype(F32)).astype(BF16)
    return x1, x2, x3


def _sel_left(m01, x):
    x1, x2, x3 = _split3(x)
    return _dot(m01, x1) + (_dot(m01, x2) + _dot(m01, x3))


def _sel_right(x, m01):
    x1, x2, x3 = _split3(x)
    return _dot(x1, m01) + (_dot(x2, m01) + _dot(x3, m01))


def _sigmoid(x):
    return 1.0 / (1.0 + jnp.exp(-x))


def _silu(x):
    return x * _sigmoid(x)


def _rms(x, g):
    return x * lax.rsqrt(jnp.mean(x * x, axis=-1, keepdims=True) + NORM_EPS) * g


def _vmem_spec():
    return pl.BlockSpec(memory_space=pltpu.VMEM)


def _params(sem):
    return pltpu.CompilerParams(dimension_semantics=sem, vmem_limit_bytes=VMEM_LIMIT_BYTES)


def _inproj_kernel(h_ref, g_ref, w_ref, z_ref):
    y = _rms(h_ref[...], g_ref[...])
    z_ref[...] = _dot(y.astype(BF16), w_ref[...])


def _inproj(h, g, w_bf):
    return pl.pallas_call(
        _inproj_kernel,
        grid=(SEQ // ROW_TILE,),
        in_specs=[pl.BlockSpec((ROW_TILE, D_MODEL), lambda i: (i, 0)),
                  _vmem_spec(), _vmem_spec()],
        out_specs=pl.BlockSpec((ROW_TILE, N_IN), lambda i: (i, 0)),
        out_shape=jax.ShapeDtypeStruct((SEQ, N_IN), F32),
        compiler_params=_params(("arbitrary",)),
        name="inproj",
    )(h, g, w_bf)


def _hgrn_kernel(q_ref, f_ref, i_ref, g_ref, lbp_ref, gn_ref, bt_ref, bo_ref, o_ref,
                 st_s, qe_s, ke_s, b_s, qq_s, kk_s, dec_s, vt_s, *, layer):
    @pl.when(pl.program_id(1) == 0)
    def _():
        st_s[...] = jnp.zeros_like(st_s)

    lbp = lbp_ref[...]
    e = jnp.exp(lbp - jnp.max(lbp, axis=0, keepdims=True))
    sm = e / jnp.sum(e, axis=0, keepdims=True)
    lb = jnp.zeros((1, HG_DIM), F32)
    for j in range(1, layer + 1):
        lb = lb + sm[j:j + 1, :]

    q = _silu(q_ref[...])
    f = lb + (1.0 - lb) * _sigmoid(f_ref[...])
    kk = 1.0 - f
    lf = jnp.log(f)
    b = _sel_left(bt_ref[...], lf)
    tot = _sel_left(bo_ref[...], lf)
    qe_s[...] = q * jnp.exp(b)
    ke_s[...] = kk * jnp.exp(tot - b)
    b_s[...] = b
    qq_s[...] = q
    kk_s[...] = kk
    dec_s[...] = jnp.exp(tot)
    vt_s[...] = i_ref[...].T

    rid = lax.broadcasted_iota(jnp.int32, (HG_BLK, HG_DIM), 0)
    trow = lax.broadcasted_iota(jnp.int32, (HG_TILE, HG_DIM), 0)

    def body(n, carry):
        r0 = pl.multiple_of(n * HG_BLK, HG_BLK)
        rows = pl.ds(r0, HG_BLK)
        st = st_s[...]
        o = _dot(qe_s[rows, :].astype(BF16), st.astype(BF16), NT)
        b_n = b_s[rows, :]
        q_n = qq_s[rows, :]
        k_n = kk_s[rows, :]
        v_n = i_ref[rows, :]
        od = jnp.zeros((HG_BLK, HG_DIM), F32)
        for i in range(HG_BLK):
            wgt = jnp.where(rid <= i, jnp.exp(b_n[i:i + 1, :] - b_n) * (q_n[i:i + 1, :] * k_n), 0.0)
            s = jnp.sum(wgt, axis=-1, keepdims=True)
            oi = jnp.sum(s * v_n, axis=0, keepdims=True)
            od = jnp.where(rid == i, oi, od)
        o_ref[rows, :] = o + od
        kem = jnp.where((trow >= r0) & (trow < r0 + HG_BLK), ke_s[...], 0.0)
        upd = _dot(vt_s[...].astype(BF16), kem.astype(BF16))
        st_s[...] = st * dec_s[pl.ds(r0, 1), :] + upd
        return carry

    lax.fori_loop(0, HG_TILE // HG_BLK, body, 0)

    o = o_ref[...]
    o = o * lax.rsqrt(jnp.mean(o * o, axis=-1, keepdims=True) + NORM_EPS)
    o_ref[...] = o * gn_ref[...] * _silu(g_ref[...])


def _hgrn(z, lb_params, gnorm_g, layer):
    t = np.arange(HG_TILE)
    same = (t[:, None] // HG_BLK) == (t[None, :] // HG_BLK)
    bt = jnp.asarray(same & (t[None, :] <= t[:, None]), BF16)
    bo = jnp.asarray(same, BF16)
    col = lambda off: pl.BlockSpec((HG_TILE, HG_DIM), lambda h, i: (i, off + h))
    scr = lambda shape: pltpu.VMEM(shape, F32)
    return pl.pallas_call(
        functools.partial(_hgrn_kernel, layer=layer),
        grid=(HG_HEADS, SEQ // HG_TILE),
        in_specs=[col(0), col(HG_HEADS), col(2 * HG_HEADS), col(3 * HG_HEADS),
                  pl.BlockSpec((DEPTH, HG_DIM), lambda h, i: (0, h)),
                  pl.BlockSpec((1, HG_DIM), lambda h, i: (0, h)),
                  _vmem_spec(), _vmem_spec()],
        out_specs=pl.BlockSpec((HG_TILE, HG_DIM), lambda h, i: (i, h)),
        out_shape=jax.ShapeDtypeStruct((SEQ, HG_WIDTH), F32),
        scratch_shapes=[scr((HG_DIM, HG_DIM))] + [scr((HG_TILE, HG_DIM))] * 6 + [scr((HG_DIM, HG_TILE))],
        compiler_params=_params(("arbitrary", "arbitrary")),
        name="hgrn2",
    )(z, z, z, z, lb_params, gnorm_g, bt, bo)


def _attn_kernel(*refs):
    nh = SW_WIDTH // LANES
    q_ref, ko_ref, vo_ref, kp_ref, vp_ref = (refs[j * nh:(j + 1) * nh] for j in range(5))
    hm_ref, o_ref = refs[5 * nh], refs[5 * nh + 1]
    acc_s, m_s, l_s = (refs[5 * nh + 2 + j * nh:5 * nh + 2 + (j + 1) * nh] for j in range(3))

    def ld(halves, rows):
        return jnp.concatenate([h[rows, :] for h in halves], axis=1)

    def st(halves, rows, val):
        for j, h in enumerate(halves):
            h[rows, :] = val[:, j * LANES:(j + 1) * LANES]

    B = SW_BLOCK
    i = pl.program_id(0)
    hm = hm_ref[...]
    shape = (SW_HEADS * B, 2 * B)
    qi = lax.broadcasted_iota(jnp.int32, shape, 0) & (B - 1)
    kj = lax.broadcasted_iota(jnp.int32, shape, 1)
    band = (kj >= qi) & (kj <= qi + B)
    own_half = kj >= B
    lane_head = lax.broadcasted_iota(jnp.int32, (B, SW_WIDTH), 1) // SW_HDIM

    for pidx, (window, dil) in enumerate(SW_PATTERNS):
        per_res = AT_SPAN // (dil * B)

        def unit(u, carry, pidx=pidx, dil=dil, per_res=per_res):
            r = u // per_res
            mb = u % per_res
            start = r + dil * B * mb

            def rows_at(s):
                if dil == 1:
                    return pl.ds(pl.multiple_of(s, B), B)
                return pl.ds(s, B, stride=dil)

            rows = rows_at(start)
            q = ld(q_ref, rows) * (SW_HDIM ** -0.5)
            prow = pl.ds(AT_SPAN - dil * B + r, B, stride=dil) if dil > 1 else pl.ds(AT_SPAN - B, B)
            k_prev = ld(kp_ref, prow)
            v_prev = ld(vp_ref, prow)
            if per_res > 1:
                inner = mb > 0
                srows = rows_at(jnp.where(inner, start - dil * B, start))
                k_prev = jnp.where(inner, ld(ko_ref, srows), k_prev)
                v_prev = jnp.where(inner, ld(vo_ref, srows), v_prev)
                valid = band & (own_half | inner | (i > 0))
            else:
                valid = band & (own_half | (i > 0))
            kw = jnp.concatenate([k_prev, ld(ko_ref, rows)], axis=0).astype(BF16)
            vw = jnp.concatenate([v_prev, ld(vo_ref, rows)], axis=0).astype(BF16)
            q4 = (jnp.concatenate([q] * SW_HEADS, axis=0) * hm).astype(BF16)
            s = jnp.where(valid, _dot(q4, kw, NT), ATTN_MASKED)
            m = jnp.max(s, axis=-1, keepdims=True)
            pe = jnp.exp(s - m)
            l = jnp.sum(pe, axis=-1, keepdims=True)
            pv = _dot(pe.astype(BF16), vw)
            acc = pv[0:B, :]
            mf = jnp.broadcast_to(m[0:B, :], (B, SW_WIDTH))
            lf = jnp.broadcast_to(l[0:B, :], (B, SW_WIDTH))
            for h in range(1, SW_HEADS):
                hs = slice(h * B, (h + 1) * B)
                sel = lane_head == h
                acc = jnp.where(sel, pv[hs, :], acc)
                mf = jnp.where(sel, m[hs, :], mf)
                lf = jnp.where(sel, l[hs, :], lf)
            if pidx > 0:
                m_old = ld(m_s, rows)
                m_new = jnp.maximum(m_old, mf)
                a_old = jnp.exp(m_old - m_new)
                a_cur = jnp.exp(mf - m_new)
                acc = ld(acc_s, rows) * a_old + acc * a_cur
                lf = ld(l_s, rows) * a_old + lf * a_cur
                mf = m_new
            st(acc_s, rows, acc)
            st(m_s, rows, mf)
            st(l_s, rows, lf)
            return carry

        lax.fori_loop(0, AT_SPAN // B, unit, 0)

    for j in range(nh):
        o_ref[:, j * LANES:(j + 1) * LANES] = acc_s[j][...] / l_s[j][...]


def _dilated_mix(z):
    for window, dil in SW_PATTERNS:
        assert window // dil == SW_BLOCK and AT_SPAN % (dil * SW_BLOCK) == 0
    nh = SW_WIDTH // LANES
    qcol = (HG_HEADS * 4 * HG_DIM) // LANES
    own = lambda c: pl.BlockSpec((AT_SPAN, LANES), lambda i: (i, qcol + c))
    prev = lambda c: pl.BlockSpec((AT_SPAN, LANES), lambda i: (jnp.maximum(i - 1, 0), qcol + c))
    row_head = np.arange(SW_HEADS * SW_BLOCK) // SW_BLOCK
    lane_head = np.arange(SW_WIDTH) // SW_HDIM
    hm = jnp.asarray(row_head[:, None] == lane_head[None, :], F32)
    scr = pltpu.VMEM((AT_SPAN, LANES), F32)
    return pl.pallas_call(
        _attn_kernel,
        grid=(SEQ // AT_SPAN,),
        in_specs=[own(c) for c in range(3 * nh)] + [prev(c) for c in range(nh, 3 * nh)] + [_vmem_spec()],
        out_specs=pl.BlockSpec((AT_SPAN, SW_WIDTH), lambda i: (i, 0)),
        out_shape=jax.ShapeDtypeStruct((SEQ, SW_WIDTH), F32),
        scratch_shapes=[scr] * (3 * nh),
        compiler_params=_params(("arbitrary",)),
        name="dilated_attn",
    )(*([z] * (5 * nh)), hm)


def _rwkv_masks():
    idx = np.arange(MXU_DIM)
    head, step = idx // RK_CHUNK, idx % RK_CHUNK
    same = head[:, None] == head[None, :]
    masks = [same, same & (step[None, :] < step[:, None]), same & (step[None, :] <= step[:, None]),
             idx[:, None] == idx[None, :]]
    for s in range(RK_LEVELS):
        a, b = idx >> s, idx >> (s + 1)
        masks.append((b[:, None] == b[None, :]) & ((a[:, None] & 1) == 1) & ((a[None, :] & 1) == 0))
    return np.stack(masks).astype(np.float32)


M_SAME, M_STRICT, M_LOWER, M_EYE, M_LEVEL0 = 0, 1, 2, 3, 4


def _rwkv_kernel(*refs, has_vres):
    (zr_ref, zk_ref, zv_ref, zl_ref, mu_ref, w0_ref, a0_ref, w2_ref, a2_ref, g2_ref,
     kkw_ref, kaw_ref, rkw_ref, lnw_ref, lnb_ref, bd_ref, tri_ref, msk_ref) = refs[:18]
    refs = refs[18:]
    if has_vres:
        vf_ref, v0_ref, v1_ref, v2_ref = refs[:4]
        refs = refs[4:]
        y_ref, st_s, cb_s = refs
    else:
        y_ref, vfo_ref, st_s, cb_s = refs
    C, T = RK_CHUNK, RK_TILE

    @pl.when(pl.program_id(0) == 0)
    def _():
        st_s[...] = jnp.zeros_like(st_s)
        cb_s[...] = jnp.zeros_like(cb_s)

    def shift_mix(idx, ref):
        c = ref[...]
        cb_s[idx, 8:8 + T, :] = c
        cp = cb_s[idx, 7:7 + T, :]
        cb_s[idx, 0:8, :] = c[T - 8:T, :]
        return c + (cp - c) * mu_ref[idx:idx + 1, :]

    r = shift_mix(0, zr_ref)
    k0 = shift_mix(1, zk_ref)
    v = shift_mix(2, zv_ref)
    lo = shift_mix(3, zl_ref)

    bd = bd_ref[...]
    wpre = w0_ref[...] + _mm(jnp.tanh(lo), w2_ref[...])
    sp = jnp.maximum(-wpre, 0.0) + jnp.log(1.0 + jnp.exp(-jnp.abs(wpre)))
    logw = -jnp.exp(-sp - 0.5)
    a = _sigmoid(a0_ref[...] + _mm(lo, a2_ref[...]))
    g = _mm(_sigmoid(lo), g2_ref[...])
    if has_vres:
        mix = _sigmoid(v0_ref[...] + _mm(_mm(v, v1_ref[...]), v2_ref[...]))
        v = v + (vf_ref[...] - v) * mix
    else:
        vfo_ref[...] = v
    kk = k0 * kkw_ref[...]
    kk = kk / jnp.maximum(jnp.sqrt(_sel_right(kk * kk, bd)), 1e-12)
    k = k0 * (1.0 + (a - 1.0) * kaw_ref[...])
    beta = kk * a

    lw = _sel_left(tri_ref[...], logw)
    e_neg = jnp.exp(-lw)
    kap = kk * jnp.exp(lw - logw)
    bh = beta * e_neg
    kh = k * e_neg
    rh = r * jnp.exp(lw)

    same = msk_ref[M_SAME]
    gram = functools.partial(_mm, passes=RK_PASS["gram"], dims=NT)
    mmi = functools.partial(_mm, passes=RK_PASS["inv"])
    mma = functools.partial(_mm, passes=RK_PASS["apply"])
    mms = functools.partial(_mm, passes=RK_PASS["state"])

    def stack(x):
        return jnp.concatenate([x] * RK_HEADS, axis=0) * same

    def chunk_mats(c):
        rows = slice(c * C, (c + 1) * C)
        lwc = lw[(c + 1) * C - 1:(c + 1) * C, :]
        e_end = jnp.exp(lwc - lw[rows, :])
        kap4, bh4, kh4, rh4, v4 = (stack(t[rows, :]) for t in (kap, bh, kh, rh, v))
        bw4 = stack(beta[rows, :] * e_end)
        kw4 = stack(k[rows, :] * e_end)
        g_kb = gram(kap4, bh4)
        a_k = gram(kap4, kh4) * msk_ref[M_STRICT]
        a_rk = gram(rh4, kh4) * msk_ref[M_LOWER]
        a_rb = gram(rh4, bh4) * msk_ref[M_LOWER]
        x = msk_ref[M_EYE] - g_kb * msk_ref[M_LEVEL0]
        for s in range(1, RK_LEVELS):
            lm = g_kb * msk_ref[M_LEVEL0 + s]
            x = x - mmi(mmi(x, lm), x)
        m1 = mma(x, kap4)
        m2 = mma(x, mma(a_k, v4))
        q1 = rh4 - mma(a_rb, m1)
        q2 = mma(a_rk, v4) - mma(a_rb, m2)
        bw4t = bw4.T
        pt = msk_ref[M_EYE] * jnp.exp(lwc) - mma(bw4t, m1)
        rt = mma(kw4.T, v4) - mma(bw4t, m2)
        return q1, q2, pt, rt

    mats = [chunk_mats(c) for c in range(T // C)]
    st = st_s[...]
    ys = []
    for q1, q2, pt, rt in mats:
        y4 = mms(q1, st) + q2
        st = mms(pt, st) + rt
        yc = y4[0:C, :]
        for h in range(1, RK_HEADS):
            yc = yc + y4[h * C:(h + 1) * C, :]
        ys.append(yc)
    st_s[...] = st
    y = jnp.concatenate(ys, axis=0)

    inv_n = 1.0 / RK_HDIM
    mean = _sel_right(y, bd) * inv_n
    d = y - mean
    var = _sel_right(d * d, bd) * inv_n
    yn = d * lax.rsqrt(var + RK_GN_EPS) * lnw_ref[...] + lnb_ref[...]
    bonus = _sel_right(r * k * rkw_ref[...], bd) * v
    y_ref[...] = (yn + bonus) * g


def _rwkv_params(w, l):
    row = lambda a: a.reshape(1, -1)
    n_dec, n_a = w["rwkv_w2"].shape[1], w["rwkv_a2"].shape[1]
    prm = {
        "mu": w["rwkv_mu"][l].reshape(4, RK_WIDTH),
        "w0": row(w["rwkv_w0"][l]), "a0": row(w["rwkv_a0"][l]),
        "w2": _pad_rows(w["rwkv_w2"][l], 0, RK_WIDTH).astype(BF16),
        "a2": _pad_rows(w["rwkv_a2"][l], n_dec, RK_WIDTH).astype(BF16),
        "g2": _pad_rows(w["rwkv_g2"][l], n_dec + n_a, RK_WIDTH).astype(BF16),
        "k_k": row(w["rwkv_k_k"][l]), "k_a": row(w["rwkv_k_a"][l]), "r_k": row(w["rwkv_r_k"][l]),
        "ln_w": row(w["rwkv_ln_w"][l]), "ln_b": row(w["rwkv_ln_b"][l]),
    }
    if l > 0:
        v1, v2 = w["rwkv_v1"][l - 1], w["rwkv_v2"][l - 1]
        prm["v0"] = row(w["rwkv_v0"][l - 1])
        prm["v1"] = jnp.pad(v1, ((0, 0), (0, LANES - v1.shape[1]))).astype(BF16)
        prm["v2"] = _pad_rows(v2, 0, LANES).astype(BF16)
    return prm


def _rwkv(z, prm, v_first):
    has_vres = v_first is not None
    C, T = RK_CHUNK, RK_TILE
    assert RK_HEADS * C == MXU_DIM and C == RK_HDIM and (1 << RK_LEVELS) == C
    ccol = (HG_HEADS * 4 * HG_DIM + 3 * SW_WIDTH) // RK_WIDTH
    zcol = lambda off: pl.BlockSpec((T, RK_WIDTH), lambda i: (i, ccol + off))
    row = pl.BlockSpec((T, RK_WIDTH), lambda i: (i, 0))
    lane = np.arange(RK_WIDTH)
    bd = jnp.asarray((lane[:, None] // RK_HDIM) == (lane[None, :] // RK_HDIM), BF16)
    t = np.arange(T)
    tri = jnp.asarray(((t[:, None] // C) == (t[None, :] // C)) & (t[None, :] <= t[:, None]), BF16)
    args = [z, z, z, z, prm["mu"], prm["w0"], prm["a0"], prm["w2"], prm["a2"], prm["g2"],
            prm["k_k"], prm["k_a"], prm["r_k"], prm["ln_w"], prm["ln_b"], bd, tri,
            jnp.asarray(_rwkv_masks())]
    in_specs = [zcol(0), zcol(1), zcol(2), zcol(3)] + [_vmem_spec()] * 14
    if has_vres:
        args += [v_first, prm["v0"], prm["v1"], prm["v2"]]
        in_specs += [row] + [_vmem_spec()] * 3
        out_specs = row
        out_shape = jax.ShapeDtypeStruct((SEQ, RK_WIDTH), F32)
    else:
        out_specs = [row, row]
        out_shape = [jax.ShapeDtypeStruct((SEQ, RK_WIDTH), F32)] * 2
    out = pl.pallas_call(
        functools.partial(_rwkv_kernel, has_vres=has_vres),
        grid=(SEQ // T,),
        in_specs=in_specs,
        out_specs=out_specs,
        out_shape=out_shape,
        scratch_shapes=[pltpu.VMEM((MXU_DIM, MXU_DIM), F32), pltpu.VMEM((4, T + 8, RK_WIDTH), F32)],
        compiler_params=_params(("arbitrary",)),
        name="rwkv7",
    )(*args)
    if has_vres:
        return out, v_first
    return out[0], out[1]


def _post_kernel(h_ref, oa_ref, ob_ref, oc_ref, p_ref, wout_ref, gffn_ref, wup_ref, cw_ref, cb_ref,
                 wdn_ref, gple_ref, wg_ref, wp_ref, gfin_ref, out_ref,
                 hn_s, acc_s, carry_s, ubuf_s, *, final):
    T = ROW_TILE

    @pl.when(pl.program_id(0) == 0)
    def _():
        carry_s[...] = jnp.zeros_like(carry_s)

    mix = (_dot(oa_ref[...].astype(BF16), wout_ref[0:HG_WIDTH, :])
           + _dot(ob_ref[...].astype(BF16), wout_ref[HG_WIDTH:HG_WIDTH + SW_WIDTH, :])
           + _dot(oc_ref[...].astype(BF16), wout_ref[HG_WIDTH + SW_WIDTH:, :]))
    h1 = h_ref[...] + mix
    hn_s[...] = _rms(h1, gffn_ref[...]).astype(BF16)
    acc_s[...] = h1

    def conv_cols(slot, col):
        cols = pl.ds(col, FFN_COLS)
        u = _dot(hn_s[...], wup_ref[:, cols])
        ubuf_s[slot, 0:8, :] = carry_s[:, cols]
        ubuf_s[slot, 8:8 + T, :] = u
        carry_s[:, cols] = u[T - 8:T, :]
        return (cb_ref[:, cols] + cw_ref[0:1, cols] * ubuf_s[slot, 6:6 + T, :]
                + cw_ref[1:2, cols] * ubuf_s[slot, 7:7 + T, :] + cw_ref[2:3, cols] * u)

    def body(j, carry):
        c0 = pl.multiple_of(j * FFN_COLS, FFN_COLS)
        gate = conv_cols(0, c0)
        val = conv_cols(1, pl.multiple_of(c0 + D_FF, LANES))
        act = (_silu(gate) * val).astype(BF16)
        acc_s[...] += _dot(act, wdn_ref[pl.ds(c0, FFN_COLS), :])
        return carry

    lax.fori_loop(0, D_FF // FFN_COLS, body, 0)

    h2 = acc_s[...]
    gate = _sigmoid(_dot(_rms(h2, gple_ref[...]).astype(BF16), wg_ref[...]))
    pe = _dot(p_ref[...].astype(BF16), wp_ref[...])
    h3 = h2 + gate * pe
    out_ref[...] = _rms(h3, gfin_ref[...]) if final else h3


def _post(h, o_a, o_b, o_c, p_l, prm, final):
    T = ROW_TILE
    tile = lambda w: pl.BlockSpec((T, w), lambda i: (i, 0))
    return pl.pallas_call(
        functools.partial(_post_kernel, final=final),
        grid=(SEQ // T,),
        in_specs=[tile(D_MODEL), tile(HG_WIDTH), tile(SW_WIDTH), tile(RK_WIDTH), tile(PLE_DIM)]
                 + [_vmem_spec()] * 10,
        out_specs=tile(D_MODEL),
        out_shape=jax.ShapeDtypeStruct((SEQ, D_MODEL), F32),
        scratch_shapes=[pltpu.VMEM((T, D_MODEL), BF16), pltpu.VMEM((T, D_MODEL), F32),
                        pltpu.VMEM((8, 2 * D_FF), F32), pltpu.VMEM((2, T + 8, FFN_COLS), F32)],
        compiler_params=_params(("arbitrary",)),
        name="outproj_ffn_ple",
    )(h, o_a, o_b, o_c, p_l, prm["w_out"], prm["g_ffn"], prm["w_up"], prm["conv_w"], prm["conv_b"],
      prm["w_down"], prm["g_ple"], prm["w_gate"], prm["w_ple"], prm["g_final"])


def _pad_rows(w, start, total):
    out = jnp.zeros((total, w.shape[1]), w.dtype)
    return lax.dynamic_update_slice(out, w, (start, 0))


def kernel(x, p, w_in, w_out, norm_mix_g, norm_ffn_g, norm_ple_g, final_norm_g, hgrn_lower_bounds, hgrn_gnorm_g, rwkv_mu, rwkv_w0, rwkv_w2, rwkv_a0, rwkv_a2, rwkv_g2, rwkv_k_k, rwkv_k_a, rwkv_r_k, rwkv_ln_w, rwkv_ln_b, rwkv_v0, rwkv_v1, rwkv_v2, ffn_up, ffn_conv_w, ffn_conv_b, ffn_down, ple_proj, ple_gate):
    row = lambda a: a.reshape(1, -1)
    h = x.reshape(SEQ, D_MODEL)
    v_first = None
    rwkv_w = dict(rwkv_mu=rwkv_mu, rwkv_w0=rwkv_w0, rwkv_w2=rwkv_w2, rwkv_a0=rwkv_a0, rwkv_a2=rwkv_a2,
                  rwkv_g2=rwkv_g2, rwkv_k_k=rwkv_k_k, rwkv_k_a=rwkv_k_a, rwkv_r_k=rwkv_r_k,
                  rwkv_ln_w=rwkv_ln_w, rwkv_ln_b=rwkv_ln_b, rwkv_v0=rwkv_v0, rwkv_v1=rwkv_v1, rwkv_v2=rwkv_v2)
    for l in range(DEPTH):
        z = _inproj(h, row(norm_mix_g[l]), w_in[l].astype(BF16))
        o_a = _hgrn(z, hgrn_lower_bounds, row(hgrn_gnorm_g[l]), l)
        o_b = _dilated_mix(z)
        o_c, v_first = _rwkv(z, _rwkv_params(rwkv_w, l), v_first)
        post = {
            "w_out": w_out[l].astype(BF16), "g_ffn": row(norm_ffn_g[l]),
            "w_up": ffn_up[l].astype(BF16), "conv_w": ffn_conv_w[l], "conv_b": row(ffn_conv_b[l]),
            "w_down": ffn_down[l].astype(BF16), "g_ple": row(norm_ple_g[l]),
            "w_gate": ple_gate[l].astype(BF16), "w_ple": ple_proj[l].astype(BF16),
            "g_final": row(final_norm_g),
        }
        h = _post(h, o_a, o_b, o_c, p[l].reshape(SEQ, PLE_DIM), post, final=l == DEPTH - 1)
    return h.reshape(1, SEQ, D_MODEL)
```

```python
import functools

import numpy as np
import jax
import jax.numpy as jnp
from jax import lax
from jax.experimental import pallas as pl
from jax.experimental.pallas import tpu as pltpu

F32 = jnp.float32
BF16 = jnp.bfloat16

D_MODEL = 1024
SEQ = 16384
DEPTH = 2
HG_HEADS = 4
HG_DIM = 128
HG_WIDTH = 512
SW_HEADS = 4
SW_HDIM = 64
SW_WIDTH = 256
SW_PATTERNS = ((128, 1), (512, 4), (2048, 16))
SW_BLOCK = 128
RK_HEADS = 4
RK_HDIM = 64
RK_WIDTH = 256
RK_GN_EPS = 64e-5
N_IN = 3840
D_FF = 2816
PLE_DIM = 256
NORM_EPS = 1e-6

LANES = 128
MXU_DIM = 256
VMEM_LIMIT_BYTES = 56 * 1024 * 1024

ROW_TILE = 512
FFN_COLS = 256
HG_TILE = 256
HG_CHUNK = 64
HG_SAFE_EXP = 80.0
AT_SPAN = SW_BLOCK * max(d for _, d in SW_PATTERNS)
ATTN_MASKED = -1e30
RK_CHUNK = 64
RK_TILE = 512
RK_LEVELS = 6

NN = (((1,), (0,)), ((), ()))
NT = (((1,), (1,)), ((), ()))


def _dot(a, b, dims=NN):
    return lax.dot_general(a, b, dims, preferred_element_type=F32)


def _mm(a, b, dims=NN):
    return _dot(a.astype(BF16), b.astype(BF16), dims)


def _split3(x):
    x1 = x.astype(BF16)
    r1 = x - x1.astype(F32)
    x2 = r1.astype(BF16)
    x3 = (r1 - x2.astype(F32)).astype(BF16)
    return x1, x2, x3


def _sel_left(m01, x, terms=3):
    x1, x2, x3 = _split3(x)
    out = _dot(m01, x1) + _dot(m01, x2)
    return out + _dot(m01, x3) if terms == 3 else out


def _sel_right(x, m01):
    x1, x2, x3 = _split3(x)
    return _dot(x1, m01) + (_dot(x2, m01) + _dot(x3, m01))


def _sigmoid(x):
    return 1.0 / (1.0 + jnp.exp(-x))


def _silu(x):
    return x * _sigmoid(x)


def _rms(x, g):
    return x * lax.rsqrt(jnp.mean(x * x, axis=-1, keepdims=True) + NORM_EPS) * g


def _vmem_spec():
    return pl.BlockSpec(memory_space=pltpu.VMEM)


def _params(sem):
    return pltpu.CompilerParams(dimension_semantics=sem, vmem_limit_bytes=VMEM_LIMIT_BYTES)


def _inproj_kernel(h_ref, g_ref, w_ref, z_ref):
    y = _rms(h_ref[...], g_ref[...])
    z_ref[...] = _dot(y.astype(BF16), w_ref[...])


def _inproj(h, g, w_bf):
    return pl.pallas_call(
        _inproj_kernel,
        grid=(SEQ // ROW_TILE,),
        in_specs=[pl.BlockSpec((ROW_TILE, D_MODEL), lambda i: (i, 0)),
                  _vmem_spec(), _vmem_spec()],
        out_specs=pl.BlockSpec((ROW_TILE, N_IN), lambda i: (i, 0)),
        out_shape=jax.ShapeDtypeStruct((SEQ, N_IN), F32),
        compiler_params=_params(("arbitrary",)),
        name="inproj",
    )(h, g, w_bf)


def _hgrn_kernel(q_ref, f_ref, i_ref, g_ref, lbp_ref, gn_ref, tri_ref, o_ref,
                 st_s, b_s, qq_s, kk_s, oi_s, *, layer):
    C, T, D = HG_CHUNK, HG_TILE, HG_DIM

    @pl.when(pl.program_id(0) == 0)
    def _():
        st_s[...] = jnp.zeros_like(st_s)

    lbp = lbp_ref[...]
    e = jnp.exp(lbp - jnp.max(lbp, axis=0, keepdims=True))
    sm = e / jnp.sum(e, axis=0, keepdims=True)
    lb = jnp.zeros((1, HG_WIDTH), F32)
    for j in range(1, layer + 1):
        lb = lb + sm[j:j + 1, :]

    q = _silu(q_ref[...])
    f = lb + (1.0 - lb) * _sigmoid(f_ref[...])
    kk = 1.0 - f
    lf = jnp.log(f)
    b = _sel_left(tri_ref[...], lf, terms=2)
    v = i_ref[...]
    b_s[...] = b
    qq_s[...] = q
    kk_s[...] = kk
    safe = jnp.min(b) > -HG_SAFE_EXP
    qe = (q * jnp.exp(b)).astype(BF16)
    kt = (kk * jnp.exp(-jnp.maximum(b, -HG_SAFE_EXP))).astype(BF16)
    vb = v.astype(BF16)
    half = lax.broadcasted_iota(jnp.int32, (2 * C, D), 0) < C
    causal = (lax.broadcasted_iota(jnp.int32, (C, C), 1) <= lax.broadcasted_iota(jnp.int32, (C, C), 0))

    heads = [slice(h * D, (h + 1) * D) for h in range(HG_HEADS)]
    states = [st_s[h] for h in range(HG_HEADS)]
    vts = [v[:, lanes].T.astype(BF16) for lanes in heads]
    for n in range(T // C):
        rows = slice(n * C, (n + 1) * C)
        slab = slice((n // 2) * 2 * C, (n // 2 + 1) * 2 * C)
        first = half if n % 2 == 0 else jnp.logical_not(half)
        for h, lanes in enumerate(heads):
            st = states[h]
            tot = b[(n + 1) * C - 1:(n + 1) * C, lanes]
            inter = _dot(qe[rows, lanes], st.astype(BF16), NT)
            s = _dot(qe[rows, lanes], kt[rows, lanes], NT)
            intra = _dot(jnp.where(causal, s, 0.0).astype(BF16), vb[rows, lanes])
            oi_s[rows, lanes] = inter
            o_ref[rows, lanes] = inter + intra
            ke = jnp.where(first, kk[slab, lanes] * jnp.exp(tot - b[slab, lanes]), 0.0)
            states[h] = st * jnp.exp(tot) + _dot(vts[h][:, slab], ke.astype(BF16))
    for h in range(HG_HEADS):
        st_s[h] = states[h]

    @pl.when(jnp.logical_not(safe))
    def _():
        rid = lax.broadcasted_iota(jnp.int32, (C, D), 0)

        def pick(x, i):
            return jnp.sum(jnp.where(rid == i, x, 0.0), axis=0, keepdims=True)

        for h in range(HG_HEADS):
            lanes = slice(h * D, (h + 1) * D)
            for n in range(T // C):
                rows = slice(n * C, (n + 1) * C)
                b_n, q_n, k_n, v_n = b_s[rows, lanes], qq_s[rows, lanes], kk_s[rows, lanes], i_ref[rows, lanes]

                def row(i, od, b_n=b_n, q_n=q_n, k_n=k_n, v_n=v_n):
                    w = jnp.exp(pick(b_n, i) - b_n) * (pick(q_n, i) * k_n)
                    s = jnp.sum(jnp.where(rid <= i, w, 0.0), axis=-1, keepdims=True)
                    return jnp.where(rid == i, jnp.sum(s * v_n, axis=0, keepdims=True), od)

                o_ref[rows, lanes] = oi_s[rows, lanes] + lax.fori_loop(0, C, row, jnp.zeros((C, D), F32))

    gate = gn_ref[...] * _silu(g_ref[...])
    for h in range(HG_HEADS):
        lanes = slice(h * D, (h + 1) * D)
        o = o_ref[:, lanes]
        o = o * lax.rsqrt(jnp.mean(o * o, axis=-1, keepdims=True) + NORM_EPS)
        o_ref[:, lanes] = o * gate[:, lanes]


def _hgrn(z, lb_params, gnorm_g, layer):
    T, C = HG_TILE, HG_CHUNK
    assert HG_DIM == LANES and T % (2 * C) == 0
    t = np.arange(T)
    tri = jnp.asarray(((t[:, None] // C) == (t[None, :] // C)) & (t[None, :] <= t[:, None]), BF16)
    col = lambda j: pl.BlockSpec((T, HG_WIDTH), lambda i: (i, j))
    scr = pltpu.VMEM((T, HG_WIDTH), F32)
    return pl.pallas_call(
        functools.partial(_hgrn_kernel, layer=layer),
        grid=(SEQ // T,),
        in_specs=[col(0), col(1), col(2), col(3), _vmem_spec(), _vmem_spec(), _vmem_spec()],
        out_specs=pl.BlockSpec((T, HG_WIDTH), lambda i: (i, 0)),
        out_shape=jax.ShapeDtypeStruct((SEQ, HG_WIDTH), F32),
        scratch_shapes=[pltpu.VMEM((HG_HEADS, HG_DIM, HG_DIM), F32), scr, scr, scr, scr],
        compiler_params=_params(("arbitrary",)),
        name="hgrn2",
    )(z, z, z, z, lb_params, gnorm_g, tri)


def _attn_kernel(*refs):
    nh = SW_WIDTH // LANES
    q_ref, ko_ref, vo_ref, kp_ref, vp_ref = (refs[j * nh:(j + 1) * nh] for j in range(5))
    hm_ref, o_ref = refs[5 * nh], refs[5 * nh + 1]
    acc_s, m_s, l_s = (refs[5 * nh + 2 + j * nh:5 * nh + 2 + (j + 1) * nh] for j in range(3))

    def ld(halves, rows):
        return jnp.concatenate([h[rows, :] for h in halves], axis=1)

    def st(halves, rows, val):
        for j, h in enumerate(halves):
            h[rows, :] = val[:, j * LANES:(j + 1) * LANES]

    B = SW_BLOCK
    i = pl.program_id(0)
    hm = hm_ref[...]
    shape = (SW_HEADS * B, 2 * B)
    qi = lax.broadcasted_iota(jnp.int32, shape, 0) & (B - 1)
    kj = lax.broadcasted_iota(jnp.int32, shape, 1)
    band = (kj >= qi) & (kj <= qi + B)
    own_half = kj >= B
    lane_head = lax.broadcasted_iota(jnp.int32, (B, SW_WIDTH), 1) // SW_HDIM

    for pidx, (window, dil) in enumerate(SW_PATTERNS):
        per_res = AT_SPAN // (dil * B)

        def unit(u, carry, pidx=pidx, dil=dil, per_res=per_res):
            r = u // per_res
            mb = u % per_res
            start = r + dil * B * mb

            def rows_at(s):
                if dil == 1:
                    return pl.ds(pl.multiple_of(s, B), B)
                return pl.ds(s, B, stride=dil)

            rows = rows_at(start)
            q = ld(q_ref, rows) * (SW_HDIM ** -0.5)
            prow = pl.ds(AT_SPAN - dil * B + r, B, stride=dil) if dil > 1 else pl.ds(AT_SPAN - B, B)
            k_prev = ld(kp_ref, prow)
            v_prev = ld(vp_ref, prow)
            if per_res > 1:
                inner = mb > 0
                srows = rows_at(jnp.where(inner, start - dil * B, start))
                k_prev = jnp.where(inner, ld(ko_ref, srows), k_prev)
                v_prev = jnp.where(inner, ld(vo_ref, srows), v_prev)
                valid = band & (own_half | inner | (i > 0))
            else:
                valid = band & (own_half | (i > 0))
            kw = jnp.concatenate([k_prev, ld(ko_ref, rows)], axis=0).astype(BF16)
            vw = jnp.concatenate([v_prev, ld(vo_ref, rows)], axis=0).astype(BF16)
            q4 = (jnp.concatenate([q] * SW_HEADS, axis=0) * hm).astype(BF16)
            s = jnp.where(valid, _dot(q4, kw, NT), ATTN_MASKED)
            m = jnp.max(s, axis=-1, keepdims=True)
            pe = jnp.exp(s - m)
            l = jnp.sum(pe, axis=-1, keepdims=True)
            pv = _dot(pe.astype(BF16), vw)
            acc = pv[0:B, :]
            mf = jnp.broadcast_to(m[0:B, :], (B, SW_WIDTH))
            lf = jnp.broadcast_to(l[0:B, :], (B, SW_WIDTH))
            for h in range(1, SW_HEADS):
                hs = slice(h * B, (h + 1) * B)
                sel = lane_head == h
                acc = jnp.where(sel, pv[hs, :], acc)
                mf = jnp.where(sel, m[hs, :], mf)
                lf = jnp.where(sel, l[hs, :], lf)
            if pidx > 0:
                m_old = ld(m_s, rows)
                m_new = jnp.maximum(m_old, mf)
                a_old = jnp.exp(m_old - m_new)
                a_cur = jnp.exp(mf - m_new)
                acc = ld(acc_s, rows) * a_old + acc * a_cur
                lf = ld(l_s, rows) * a_old + lf * a_cur
                mf = m_new
            st(acc_s, rows, acc)
            st(m_s, rows, mf)
            st(l_s, rows, lf)
            return carry

        lax.fori_loop(0, AT_SPAN // B, unit, 0)

    for j in range(nh):
        o_ref[:, j * LANES:(j + 1) * LANES] = acc_s[j][...] / l_s[j][...]


def _dilated_mix(z):
    for window, dil in SW_PATTERNS:
        assert window // dil == SW_BLOCK and AT_SPAN % (dil * SW_BLOCK) == 0
    nh = SW_WIDTH // LANES
    qcol = (HG_HEADS * 4 * HG_DIM) // LANES
    own = lambda c: pl.BlockSpec((AT_SPAN, LANES), lambda i: (i, qcol + c))
    prev = lambda c: pl.BlockSpec((AT_SPAN, LANES), lambda i: (jnp.maximum(i - 1, 0), qcol + c))
    row_head = np.arange(SW_HEADS * SW_BLOCK) // SW_BLOCK
    lane_head = np.arange(SW_WIDTH) // SW_HDIM
    hm = jnp.asarray(row_head[:, None] == lane_head[None, :], F32)
    scr = pltpu.VMEM((AT_SPAN, LANES), F32)
    return pl.pallas_call(
        _attn_kernel,
        grid=(SEQ // AT_SPAN,),
        in_specs=[own(c) for c in range(3 * nh)] + [prev(c) for c in range(nh, 3 * nh)] + [_vmem_spec()],
        out_specs=pl.BlockSpec((AT_SPAN, SW_WIDTH), lambda i: (i, 0)),
        out_shape=jax.ShapeDtypeStruct((SEQ, SW_WIDTH), F32),
        scratch_shapes=[scr] * (3 * nh),
        compiler_params=_params(("arbitrary",)),
        name="dilated_attn",
    )(*([z] * (5 * nh)), hm)


def _rwkv_masks():
    t = np.arange(RK_CHUNK)[:, None]
    s = np.tile(np.arange(RK_CHUNK), RK_HEADS)[None, :]
    masks = [s < t, s <= t, s == t]
    for lv in range(RK_LEVELS):
        masks.append(((t >> (lv + 1)) == (s >> (lv + 1))) & (((t >> lv) & 1) == 1) & (((s >> lv) & 1) == 0))
    return np.stack(masks).astype(np.float32)


M_STRICT, M_LOWER, M_EYE, M_LEVEL0 = 0, 1, 2, 3


def _rwkv_kernel(*refs, has_vres):
    (zr_ref, zk_ref, zv_ref, zl_ref, mu_ref, w0_ref, a0_ref, w2_ref, a2_ref, g2_ref,
     kkw_ref, kaw_ref, rkw_ref, lnw_ref, lnb_ref, bd_ref, tri_ref, msk_ref) = refs[:18]
    refs = refs[18:]
    if has_vres:
        vf_ref, v0_ref, v1_ref, v2_ref = refs[:4]
        refs = refs[4:]
        y_ref, st_s, cb_s = refs
    else:
        y_ref, vfo_ref, st_s, cb_s = refs
    C, T = RK_CHUNK, RK_TILE

    @pl.when(pl.program_id(0) == 0)
    def _():
        st_s[...] = jnp.zeros_like(st_s)
        cb_s[...] = jnp.zeros_like(cb_s)

    def shift_mix(idx, ref):
        c = ref[...]
        cb_s[idx, 8:8 + T, :] = c
        cp = cb_s[idx, 7:7 + T, :]
        cb_s[idx, 0:8, :] = c[T - 8:T, :]
        return c + (cp - c) * mu_ref[idx:idx + 1, :]

    r = shift_mix(0, zr_ref)
    k0 = shift_mix(1, zk_ref)
    v = shift_mix(2, zv_ref)
    lo = shift_mix(3, zl_ref)

    bd = bd_ref[...]
    wpre = w0_ref[...] + _mm(jnp.tanh(lo), w2_ref[...])
    sp = jnp.maximum(-wpre, 0.0) + jnp.log(1.0 + jnp.exp(-jnp.abs(wpre)))
    logw = -jnp.exp(-sp - 0.5)
    a = _sigmoid(a0_ref[...] + _mm(lo, a2_ref[...]))
    g = _mm(_sigmoid(lo), g2_ref[...])
    if has_vres:
        mix = _sigmoid(v0_ref[...] + _mm(_mm(v, v1_ref[...]), v2_ref[...]))
        v = v + (vf_ref[...] - v) * mix
    else:
        vfo_ref[...] = v
    kk = k0 * kkw_ref[...]
    kk = kk / jnp.maximum(jnp.sqrt(_sel_right(kk * kk, bd)), 1e-12)
    k = k0 * (1.0 + (a - 1.0) * kaw_ref[...])
    beta = kk * a

    lw = _sel_left(tri_ref[...], logw)
    e_neg = jnp.exp(-lw)
    kap = kk * jnp.exp(lw - logw)
    bh = beta * e_neg
    kh = k * e_neg
    rh = r * jnp.exp(lw)
    bdf = bd.astype(F32)

    def stack(x):
        return jnp.concatenate([x.astype(BF16)] * RK_HEADS, axis=0) * bd

    def packed_t(x):
        full = (jnp.concatenate([x] * RK_HEADS, axis=0) * bdf).T
        out = full[0:C, :]
        for h in range(1, RK_HEADS):
            out = out + full[h * C:(h + 1) * C, :]
        return out

    def mm(x, w, dims=NN):
        return _dot(x.astype(BF16), w, dims)

    chunks = range(T // C)
    rows = [slice(c * C, (c + 1) * C) for c in chunks]
    each = lambda fn, *lists: [fn(*args) for args in zip(*lists)]
    lwc = [lw[(c + 1) * C - 1:(c + 1) * C, :] for c in chunks]
    e_end = [jnp.exp(lwc[c] - lw[rows[c], :]) for c in chunks]
    kap_c = [kap[rw, :] for rw in rows]
    rh_c = [rh[rw, :] for rw in rows]
    bh4 = [stack(bh[rw, :]) for rw in rows]
    kh4 = [stack(kh[rw, :]) for rw in rows]
    v4 = [stack(v[rw, :]) for rw in rows]
    g_kb = each(lambda x, w: mm(x, w, NT), kap_c, bh4)
    a_k = each(lambda x, w: mm(x, w, NT) * msk_ref[M_STRICT], kap_c, kh4)
    a_rk = each(lambda x, w: mm(x, w, NT) * msk_ref[M_LOWER], rh_c, kh4)
    a_rb = each(lambda x, w: mm(x, w, NT) * msk_ref[M_LOWER], rh_c, bh4)
    x = [msk_ref[M_EYE] - gc * msk_ref[M_LEVEL0] for gc in g_kb]
    for lv in range(1, RK_LEVELS):
        xl = each(lambda xc, gc: mm(xc, stack(gc * msk_ref[M_LEVEL0 + lv])), x, g_kb)
        x = each(lambda xc, xlc: xc - mm(xlc, stack(xc)), x, xl)
    m1 = each(lambda xc, kc: mm(xc, stack(kc)), x, kap_c)
    akv = each(mm, a_k, v4)
    m2 = each(lambda xc, ac: mm(xc, stack(ac)), x, akv)
    m14 = [stack(t) for t in m1]
    m24 = [stack(t) for t in m2]
    q1 = each(lambda rc, ac, mc: rc - mm(ac, mc), rh_c, a_rb, m14)
    q2 = each(lambda ark, vc, arb, mc: mm(ark, vc) - mm(arb, mc), a_rk, v4, a_rb, m24)
    bwt = [packed_t(beta[rows[c], :] * e_end[c]) for c in chunks]
    kwt = [packed_t(k[rows[c], :] * e_end[c]) for c in chunks]
    pt = each(lambda lc, bc, mc: msk_ref[M_EYE] * jnp.exp(lc) - mm(bc, mc), lwc, bwt, m14)
    rt = each(lambda kc, vc, bc, mc: mm(kc, vc) - mm(bc, mc), kwt, v4, bwt, m24)

    st = st_s[...]
    ys = []
    for c in chunks:
        st4 = stack(st)
        ys.append(mm(q1[c], st4) + q2[c])
        st = mm(pt[c], st4) + rt[c]
    st_s[...] = st
    y = jnp.concatenate(ys, axis=0)

    inv_n = 1.0 / RK_HDIM
    mean = _sel_right(y, bd) * inv_n
    d = y - mean
    var = _sel_right(d * d, bd) * inv_n
    yn = d * lax.rsqrt(var + RK_GN_EPS) * lnw_ref[...] + lnb_ref[...]
    bonus = _sel_right(r * k * rkw_ref[...], bd) * v
    y_ref[...] = (yn + bonus) * g


def _rwkv_params(w, l):
    row = lambda a: a.reshape(1, -1)
    n_dec, n_a = w["rwkv_w2"].shape[1], w["rwkv_a2"].shape[1]
    prm = {
        "mu": w["rwkv_mu"][l].reshape(4, RK_WIDTH),
        "w0": row(w["rwkv_w0"][l]), "a0": row(w["rwkv_a0"][l]),
        "w2": _pad_rows(w["rwkv_w2"][l], 0, RK_WIDTH).astype(BF16),
        "a2": _pad_rows(w["rwkv_a2"][l], n_dec, RK_WIDTH).astype(BF16),
        "g2": _pad_rows(w["rwkv_g2"][l], n_dec + n_a, RK_WIDTH).astype(BF16),
        "k_k": row(w["rwkv_k_k"][l]), "k_a": row(w["rwkv_k_a"][l]), "r_k": row(w["rwkv_r_k"][l]),
        "ln_w": row(w["rwkv_ln_w"][l]), "ln_b": row(w["rwkv_ln_b"][l]),
    }
    if l > 0:
        v1, v2 = w["rwkv_v1"][l - 1], w["rwkv_v2"][l - 1]
        prm["v0"] = row(w["rwkv_v0"][l - 1])
        prm["v1"] = jnp.pad(v1, ((0, 0), (0, LANES - v1.shape[1]))).astype(BF16)
        prm["v2"] = _pad_rows(v2, 0, LANES).astype(BF16)
    return prm


def _rwkv(z, prm, v_first):
    has_vres = v_first is not None
    C, T = RK_CHUNK, RK_TILE
    assert RK_HEADS * C == MXU_DIM and C == RK_HDIM and (1 << RK_LEVELS) == C
    ccol = (HG_HEADS * 4 * HG_DIM + 3 * SW_WIDTH) // RK_WIDTH
    zcol = lambda off: pl.BlockSpec((T, RK_WIDTH), lambda i: (i, ccol + off))
    row = pl.BlockSpec((T, RK_WIDTH), lambda i: (i, 0))
    lane = np.arange(RK_WIDTH)
    bd = jnp.asarray((lane[:, None] // RK_HDIM) == (lane[None, :] // RK_HDIM), BF16)
    t = np.arange(T)
    tri = jnp.asarray(((t[:, None] // C) == (t[None, :] // C)) & (t[None, :] <= t[:, None]), BF16)
    args = [z, z, z, z, prm["mu"], prm["w0"], prm["a0"], prm["w2"], prm["a2"], prm["g2"],
            prm["k_k"], prm["k_a"], prm["r_k"], prm["ln_w"], prm["ln_b"], bd, tri,
            jnp.asarray(_rwkv_masks())]
    in_specs = [zcol(0), zcol(1), zcol(2), zcol(3)] + [_vmem_spec()] * 14
    if has_vres:
        args += [v_first, prm["v0"], prm["v1"], prm["v2"]]
        in_specs += [row] + [_vmem_spec()] * 3
        out_specs = row
        out_shape = jax.ShapeDtypeStruct((SEQ, RK_WIDTH), F32)
    else:
        out_specs = [row, row]
        out_shape = [jax.ShapeDtypeStruct((SEQ, RK_WIDTH), F32)] * 2
    out = pl.pallas_call(
        functools.partial(_rwkv_kernel, has_vres=has_vres),
        grid=(SEQ // T,),
        in_specs=in_specs,
        out_specs=out_specs,
        out_shape=out_shape,
        scratch_shapes=[pltpu.VMEM((RK_HDIM, RK_WIDTH), F32), pltpu.VMEM((4, T + 8, RK_WIDTH), F32)],
        compiler_params=_params(("arbitrary",)),
        name="rwkv7",
    )(*args)
    if has_vres:
        return out, v_first
    return out[0], out[1]


def _post_kernel(h_ref, oa_ref, ob_ref, oc_ref, p_ref, wout_ref, gffn_ref, wup_ref, cw_ref, cb_ref,
                 wdn_ref, gple_ref, wg_ref, wp_ref, gfin_ref, out_ref,
                 hn_s, acc_s, carry_s, ubuf_s, *, final):
    T = ROW_TILE

    @pl.when(pl.program_id(0) == 0)
    def _():
        carry_s[...] = jnp.zeros_like(carry_s)

    mix = (_dot(oa_ref[...].astype(BF16), wout_ref[0:HG_WIDTH, :])
           + _dot(ob_ref[...].astype(BF16), wout_ref[HG_WIDTH:HG_WIDTH + SW_WIDTH, :])
           + _dot(oc_ref[...].astype(BF16), wout_ref[HG_WIDTH + SW_WIDTH:, :]))
    h1 = h_ref[...] + mix
    hn_s[...] = _rms(h1, gffn_ref[...]).astype(BF16)
    acc_s[...] = h1

    def conv_cols(slot, col):
        cols = pl.ds(col, FFN_COLS)
        u = _dot(hn_s[...], wup_ref[:, cols])
        ubuf_s[slot, 0:8, :] = carry_s[:, cols]
        ubuf_s[slot, 8:8 + T, :] = u
        carry_s[:, cols] = u[T - 8:T, :]
        return (cb_ref[:, cols] + cw_ref[0:1, cols] * ubuf_s[slot, 6:6 + T, :]
                + cw_ref[1:2, cols] * ubuf_s[slot, 7:7 + T, :] + cw_ref[2:3, cols] * u)

    def body(j, carry):
        c0 = pl.multiple_of(j * FFN_COLS, FFN_COLS)
        gate = conv_cols(0, c0)
        val = conv_cols(1, pl.multiple_of(c0 + D_FF, LANES))
        act = (_silu(gate) * val).astype(BF16)
        acc_s[...] += _dot(act, wdn_ref[pl.ds(c0, FFN_COLS), :])
        return carry

    lax.fori_loop(0, D_FF // FFN_COLS, body, 0)

    h2 = acc_s[...]
    gate = _sigmoid(_dot(_rms(h2, gple_ref[...]).astype(BF16), wg_ref[...]))
    pe = _dot(p_ref[...].astype(BF16), wp_ref[...])
    h3 = h2 + gate * pe
    out_ref[...] = _rms(h3, gfin_ref[...]) if final else h3


def _post(h, o_a, o_b, o_c, p_l, prm, final):
    T = ROW_TILE
    tile = lambda w: pl.BlockSpec((T, w), lambda i: (i, 0))
    return pl.pallas_call(
        functools.partial(_post_kernel, final=final),
        grid=(SEQ // T,),
        in_specs=[tile(D_MODEL), tile(HG_WIDTH), tile(SW_WIDTH), tile(RK_WIDTH), tile(PLE_DIM)]
                 + [_vmem_spec()] * 10,
        out_specs=tile(D_MODEL),
        out_shape=jax.ShapeDtypeStruct((SEQ, D_MODEL), F32),
        scratch_shapes=[pltpu.VMEM((T, D_MODEL), BF16), pltpu.VMEM((T, D_MODEL), F32),
                        pltpu.VMEM((8, 2 * D_FF), F32), pltpu.VMEM((2, T + 8, FFN_COLS), F32)],
        compiler_params=_params(("arbitrary",)),
        name="outproj_ffn_ple",
    )(h, o_a, o_b, o_c, p_l, prm["w_out"], prm["g_ffn"], prm["w_up"], prm["conv_w"], prm["conv_b"],
      prm["w_down"], prm["g_ple"], prm["w_gate"], prm["w_ple"], prm["g_final"])


def _pad_rows(w, start, total):
    out = jnp.zeros((total, w.shape[1]), w.dtype)
    return lax.dynamic_update_slice(out, w, (start, 0))


def kernel(x, p, w_in, w_out, norm_mix_g, norm_ffn_g, norm_ple_g, final_norm_g, hgrn_lower_bounds, hgrn_gnorm_g, rwkv_mu, rwkv_w0, rwkv_w2, rwkv_a0, rwkv_a2, rwkv_g2, rwkv_k_k, rwkv_k_a, rwkv_r_k, rwkv_ln_w, rwkv_ln_b, rwkv_v0, rwkv_v1, rwkv_v2, ffn_up, ffn_conv_w, ffn_conv_b, ffn_down, ple_proj, ple_gate):
    row = lambda a: a.reshape(1, -1)
    h = x.reshape(SEQ, D_MODEL)
    v_first = None
    rwkv_w = dict(rwkv_mu=rwkv_mu, rwkv_w0=rwkv_w0, rwkv_w2=rwkv_w2, rwkv_a0=rwkv_a0, rwkv_a2=rwkv_a2,
                  rwkv_g2=rwkv_g2, rwkv_k_k=rwkv_k_k, rwkv_k_a=rwkv_k_a, rwkv_r_k=rwkv_r_k,
                  rwkv_ln_w=rwkv_ln_w, rwkv_ln_b=rwkv_ln_b, rwkv_v0=rwkv_v0, rwkv_v1=rwkv_v1, rwkv_v2=rwkv_v2)
    for l in range(DEPTH):
        z = _inproj(h, row(norm_mix_g[l]), w_in[l].astype(BF16))
        o_a = _hgrn(z, hgrn_lower_bounds, row(hgrn_gnorm_g[l]), l)
        o_b = _dilated_mix(z)
        o_c, v_first = _rwkv(z, _rwkv_params(rwkv_w, l), v_first)
        post = {
            "w_out": w_out[l].astype(BF16), "g_ffn": row(norm_ffn_g[l]),
            "w_up": ffn_up[l].astype(BF16), "conv_w": ffn_conv_w[l], "conv_b": row(ffn_conv_b[l]),
            "w_down": ffn_down[l].astype(BF16), "g_ple": row(norm_ple_g[l]),
            "w_gate": ple_gate[l].astype(BF16), "w_ple": ple_proj[l].astype(BF16),
            "g_final": row(final_norm_g),
        }
        h = _post(h, o_a, o_b, o_c, p[l].reshape(SEQ, PLE_DIM), post, final=l == DEPTH - 1)
    return h.reshape(1, SEQ, D_MODEL)
```

```python
import functools

import numpy as np
import jax
import jax.numpy as jnp
from jax import lax
from jax.experimental import pallas as pl
from jax.experimental.pallas import tpu as pltpu

F32 = jnp.float32
BF16 = jnp.bfloat16

D_MODEL = 1024
SEQ = 16384
DEPTH = 2
HG_HEADS = 4
HG_DIM = 128
HG_WIDTH = 512
SW_HEADS = 4
SW_HDIM = 64
SW_WIDTH = 256
SW_PATTERNS = ((128, 1), (512, 4), (2048, 16))
SW_BLOCK = 128
RK_HEADS = 4
RK_HDIM = 64
RK_WIDTH = 256
RK_GN_EPS = 64e-5
N_IN = 3840
D_FF = 2816
PLE_DIM = 256
NORM_EPS = 1e-6

LANES = 128
MXU_DIM = 256
VMEM_LIMIT_BYTES = 56 * 1024 * 1024

ROW_TILE = 512
FFN_COLS = 256
HG_TILE = 256
HG_CHUNK = 64
HG_SAFE_EXP = 80.0
AT_SPAN = SW_BLOCK * max(d for _, d in SW_PATTERNS)
ATTN_MASKED = -1e30
AT_GROUP = 4
RK_CHUNK = 64
RK_TILE = 512
RK_LEVELS = 6

NN = (((1,), (0,)), ((), ()))
NT = (((1,), (1,)), ((), ()))


def _dot(a, b, dims=NN):
    return lax.dot_general(a, b, dims, preferred_element_type=F32)


def _mm(a, b, dims=NN):
    return _dot(a.astype(BF16), b.astype(BF16), dims)


def _split3(x):
    x1 = x.astype(BF16)
    r1 = x - x1.astype(F32)
    x2 = r1.astype(BF16)
    x3 = (r1 - x2.astype(F32)).astype(BF16)
    return x1, x2, x3


def _sel_left(m01, x, terms=3):
    x1, x2, x3 = _split3(x)
    out = _dot(m01, x1) + _dot(m01, x2)
    return out + _dot(m01, x3) if terms == 3 else out


def _sel_right(x, m01):
    x1, x2, x3 = _split3(x)
    return _dot(x1, m01) + (_dot(x2, m01) + _dot(x3, m01))


def _sigmoid(x):
    return 1.0 / (1.0 + jnp.exp(-x))


def _silu(x):
    return x * _sigmoid(x)


def _rms(x, g):
    return x * lax.rsqrt(jnp.mean(x * x, axis=-1, keepdims=True) + NORM_EPS) * g


def _vmem_spec():
    return pl.BlockSpec(memory_space=pltpu.VMEM)


def _params(sem):
    return pltpu.CompilerParams(dimension_semantics=sem, vmem_limit_bytes=VMEM_LIMIT_BYTES)


def _inproj_kernel(h_ref, g_ref, w_ref, z_ref):
    y = _rms(h_ref[...], g_ref[...])
    z_ref[...] = _dot(y.astype(BF16), w_ref[...])


def _inproj(h, g, w_bf):
    return pl.pallas_call(
        _inproj_kernel,
        grid=(SEQ // ROW_TILE,),
        in_specs=[pl.BlockSpec((ROW_TILE, D_MODEL), lambda i: (i, 0)),
                  _vmem_spec(), _vmem_spec()],
        out_specs=pl.BlockSpec((ROW_TILE, N_IN), lambda i: (i, 0)),
        out_shape=jax.ShapeDtypeStruct((SEQ, N_IN), F32),
        compiler_params=_params(("arbitrary",)),
        name="inproj",
    )(h, g, w_bf)


def _hgrn_kernel(q_ref, f_ref, i_ref, g_ref, lbp_ref, gn_ref, tri_ref, o_ref,
                 st_s, b_s, qq_s, kk_s, oi_s, *, layer):
    C, T, D = HG_CHUNK, HG_TILE, HG_DIM

    @pl.when(pl.program_id(0) == 0)
    def _():
        st_s[...] = jnp.zeros_like(st_s)

    lbp = lbp_ref[...]
    e = jnp.exp(lbp - jnp.max(lbp, axis=0, keepdims=True))
    sm = e / jnp.sum(e, axis=0, keepdims=True)
    lb = jnp.zeros((1, HG_WIDTH), F32)
    for j in range(1, layer + 1):
        lb = lb + sm[j:j + 1, :]

    q = _silu(q_ref[...])
    f = lb + (1.0 - lb) * _sigmoid(f_ref[...])
    kk = 1.0 - f
    lf = jnp.log(f)
    b = _sel_left(tri_ref[...], lf, terms=2)
    v = i_ref[...]
    b_s[...] = b
    qq_s[...] = q
    kk_s[...] = kk
    safe = jnp.min(b) > -HG_SAFE_EXP
    qe = (q * jnp.exp(b)).astype(BF16)
    kt = (kk * jnp.exp(-jnp.maximum(b, -HG_SAFE_EXP))).astype(BF16)
    vb = v.astype(BF16)
    half = lax.broadcasted_iota(jnp.int32, (2 * C, D), 0) < C
    causal = (lax.broadcasted_iota(jnp.int32, (C, C), 1) <= lax.broadcasted_iota(jnp.int32, (C, C), 0))

    heads = [slice(h * D, (h + 1) * D) for h in range(HG_HEADS)]
    states = [st_s[h] for h in range(HG_HEADS)]
    vts = [v[:, lanes].T.astype(BF16) for lanes in heads]
    for n in range(T // C):
        rows = slice(n * C, (n + 1) * C)
        slab = slice((n // 2) * 2 * C, (n // 2 + 1) * 2 * C)
        first = half if n % 2 == 0 else jnp.logical_not(half)
        for h, lanes in enumerate(heads):
            st = states[h]
            tot = b[(n + 1) * C - 1:(n + 1) * C, lanes]
            inter = _dot(qe[rows, lanes], st.astype(BF16), NT)
            s = _dot(qe[rows, lanes], kt[rows, lanes], NT)
            intra = _dot(jnp.where(causal, s, 0.0).astype(BF16), vb[rows, lanes])
            oi_s[rows, lanes] = inter
            o_ref[rows, lanes] = inter + intra
            ke = jnp.where(first, kk[slab, lanes] * jnp.exp(tot - b[slab, lanes]), 0.0)
            states[h] = st * jnp.exp(tot) + _dot(vts[h][:, slab], ke.astype(BF16))
    for h in range(HG_HEADS):
        st_s[h] = states[h]

    @pl.when(jnp.logical_not(safe))
    def _():
        rid = lax.broadcasted_iota(jnp.int32, (C, D), 0)

        def pick(x, i):
            return jnp.sum(jnp.where(rid == i, x, 0.0), axis=0, keepdims=True)

        for h in range(HG_HEADS):
            lanes = slice(h * D, (h + 1) * D)
            for n in range(T // C):
                rows = slice(n * C, (n + 1) * C)
                b_n, q_n, k_n, v_n = b_s[rows, lanes], qq_s[rows, lanes], kk_s[rows, lanes], i_ref[rows, lanes]

                def row(i, od, b_n=b_n, q_n=q_n, k_n=k_n, v_n=v_n):
                    w = jnp.exp(pick(b_n, i) - b_n) * (pick(q_n, i) * k_n)
                    s = jnp.sum(jnp.where(rid <= i, w, 0.0), axis=-1, keepdims=True)
                    return jnp.where(rid == i, jnp.sum(s * v_n, axis=0, keepdims=True), od)

                o_ref[rows, lanes] = oi_s[rows, lanes] + lax.fori_loop(0, C, row, jnp.zeros((C, D), F32))

    gate = gn_ref[...] * _silu(g_ref[...])
    for h in range(HG_HEADS):
        lanes = slice(h * D, (h + 1) * D)
        o = o_ref[:, lanes]
        o = o * lax.rsqrt(jnp.mean(o * o, axis=-1, keepdims=True) + NORM_EPS)
        o_ref[:, lanes] = o * gate[:, lanes]


def _hgrn(z, lb_params, gnorm_g, layer):
    T, C = HG_TILE, HG_CHUNK
    assert HG_DIM == LANES and T % (2 * C) == 0
    t = np.arange(T)
    tri = jnp.asarray(((t[:, None] // C) == (t[None, :] // C)) & (t[None, :] <= t[:, None]), BF16)
    col = lambda j: pl.BlockSpec((T, HG_WIDTH), lambda i: (i, j))
    scr = pltpu.VMEM((T, HG_WIDTH), F32)
    return pl.pallas_call(
        functools.partial(_hgrn_kernel, layer=layer),
        grid=(SEQ // T,),
        in_specs=[col(0), col(1), col(2), col(3), _vmem_spec(), _vmem_spec(), _vmem_spec()],
        out_specs=pl.BlockSpec((T, HG_WIDTH), lambda i: (i, 0)),
        out_shape=jax.ShapeDtypeStruct((SEQ, HG_WIDTH), F32),
        scratch_shapes=[pltpu.VMEM((HG_HEADS, HG_DIM, HG_DIM), F32), scr, scr, scr, scr],
        compiler_params=_params(("arbitrary",)),
        name="hgrn2",
    )(z, z, z, z, lb_params, gnorm_g, tri)


def _attn_kernel(*refs):
    nh = SW_WIDTH // LANES
    q_ref, ko_ref, vo_ref, kp_ref, vp_ref = (refs[j * nh:(j + 1) * nh] for j in range(5))
    hm_ref, o_ref = refs[5 * nh], refs[5 * nh + 1]
    acc_s, m_s, l_s = (refs[5 * nh + 2 + j * nh:5 * nh + 2 + (j + 1) * nh] for j in range(3))

    def ld(halves, rows):
        return jnp.concatenate([h[rows, :] for h in halves], axis=1)

    def st(halves, rows, val):
        for j, h in enumerate(halves):
            h[rows, :] = val[:, j * LANES:(j + 1) * LANES]

    B = SW_BLOCK
    i = pl.program_id(0)
    hm = hm_ref[...]
    shape = (SW_HEADS * B, 2 * B)
    qi = lax.broadcasted_iota(jnp.int32, shape, 0) & (B - 1)
    kj = lax.broadcasted_iota(jnp.int32, shape, 1)
    band = (kj >= qi) & (kj <= qi + B)
    own_half = kj >= B
    lane_head = lax.broadcasted_iota(jnp.int32, (B, SW_WIDTH), 1) // SW_HDIM

    for pidx, (window, dil) in enumerate(SW_PATTERNS):
        per_res = AT_SPAN // (dil * B)

        def rows_at(s, dil=dil):
            if dil == 1:
                return pl.ds(pl.multiple_of(s, B), B)
            return pl.ds(s, B, stride=dil)

        def scores(u, dil=dil, per_res=per_res):
            r = u // per_res
            mb = u % per_res
            start = r + dil * B * mb
            rows = rows_at(start)
            q = ld(q_ref, rows) * (SW_HDIM ** -0.5)
            prow = pl.ds(AT_SPAN - dil * B + r, B, stride=dil) if dil > 1 else pl.ds(AT_SPAN - B, B)
            k_prev = ld(kp_ref, prow)
            v_prev = ld(vp_ref, prow)
            if per_res > 1:
                inner = mb > 0
                srows = rows_at(jnp.where(inner, start - dil * B, start))
                k_prev = jnp.where(inner, ld(ko_ref, srows), k_prev)
                v_prev = jnp.where(inner, ld(vo_ref, srows), v_prev)
                valid = band & (own_half | inner | (i > 0))
            else:
                valid = band & (own_half | (i > 0))
            kw = jnp.concatenate([k_prev, ld(ko_ref, rows)], axis=0).astype(BF16)
            vw = jnp.concatenate([v_prev, ld(vo_ref, rows)], axis=0).astype(BF16)
            q4 = (jnp.concatenate([q] * SW_HEADS, axis=0) * hm).astype(BF16)
            return rows, jnp.where(valid, _dot(q4, kw, NT), ATTN_MASKED), vw

        def softmax(s):
            m = jnp.max(s, axis=-1, keepdims=True)
            pe = jnp.exp(s - m)
            return m, pe.astype(BF16), jnp.sum(pe, axis=-1, keepdims=True)

        def merge(rows, m, l, pv, pidx=pidx):
            acc = pv[0:B, :]
            mf = jnp.broadcast_to(m[0:B, :], (B, SW_WIDTH))
            lf = jnp.broadcast_to(l[0:B, :], (B, SW_WIDTH))
            for h in range(1, SW_HEADS):
                hs = slice(h * B, (h + 1) * B)
                sel = lane_head == h
                acc = jnp.where(sel, pv[hs, :], acc)
                mf = jnp.where(sel, m[hs, :], mf)
                lf = jnp.where(sel, l[hs, :], lf)
            if pidx > 0:
                m_old = ld(m_s, rows)
                m_new = jnp.maximum(m_old, mf)
                a_old = jnp.exp(m_old - m_new)
                a_cur = jnp.exp(mf - m_new)
                acc = ld(acc_s, rows) * a_old + acc * a_cur
                lf = ld(l_s, rows) * a_old + lf * a_cur
                mf = m_new
            return acc, mf, lf

        def group(gi, carry):
            units = [gi * AT_GROUP + t for t in range(AT_GROUP)]
            sc = [scores(u) for u in units]
            sm = [softmax(s) for _, s, _ in sc]
            pv = [_dot(pe, vw) for (_, pe, _), (_, _, vw) in zip(sm, sc)]
            out = [merge(rows, m, l, p) for (rows, _, _), (m, _, l), p in zip(sc, sm, pv)]
            for (rows, _, _), (acc, mf, lf) in zip(sc, out):
                st(acc_s, rows, acc)
                st(m_s, rows, mf)
                st(l_s, rows, lf)
            return carry

        lax.fori_loop(0, AT_SPAN // (B * AT_GROUP), group, 0)

    for j in range(nh):
        o_ref[:, j * LANES:(j + 1) * LANES] = acc_s[j][...] / l_s[j][...]


def _dilated_mix(z):
    for window, dil in SW_PATTERNS:
        assert window // dil == SW_BLOCK and AT_SPAN % (dil * SW_BLOCK) == 0
    nh = SW_WIDTH // LANES
    qcol = (HG_HEADS * 4 * HG_DIM) // LANES
    own = lambda c: pl.BlockSpec((AT_SPAN, LANES), lambda i: (i, qcol + c))
    prev = lambda c: pl.BlockSpec((AT_SPAN, LANES), lambda i: (jnp.maximum(i - 1, 0), qcol + c))
    row_head = np.arange(SW_HEADS * SW_BLOCK) // SW_BLOCK
    lane_head = np.arange(SW_WIDTH) // SW_HDIM
    hm = jnp.asarray(row_head[:, None] == lane_head[None, :], F32)
    scr = pltpu.VMEM((AT_SPAN, LANES), F32)
    return pl.pallas_call(
        _attn_kernel,
        grid=(SEQ // AT_SPAN,),
        in_specs=[own(c) for c in range(3 * nh)] + [prev(c) for c in range(nh, 3 * nh)] + [_vmem_spec()],
        out_specs=pl.BlockSpec((AT_SPAN, SW_WIDTH), lambda i: (i, 0)),
        out_shape=jax.ShapeDtypeStruct((SEQ, SW_WIDTH), F32),
        scratch_shapes=[scr] * (3 * nh),
        compiler_params=_params(("arbitrary",)),
        name="dilated_attn",
    )(*([z] * (5 * nh)), hm)


def _rwkv_masks():
    t = np.arange(RK_CHUNK)[:, None]
    s = np.tile(np.arange(RK_CHUNK), RK_HEADS)[None, :]
    masks = [s < t, s <= t, s == t]
    for lv in range(RK_LEVELS):
        masks.append(((t >> (lv + 1)) == (s >> (lv + 1))) & (((t >> lv) & 1) == 1) & (((s >> lv) & 1) == 0))
    return np.stack(masks).astype(np.float32)


M_STRICT, M_LOWER, M_EYE, M_LEVEL0 = 0, 1, 2, 3


def _rwkv_kernel(*refs, has_vres):
    (zr_ref, zk_ref, zv_ref, zl_ref, mu_ref, w0_ref, a0_ref, w2_ref, a2_ref, g2_ref,
     kkw_ref, kaw_ref, rkw_ref, lnw_ref, lnb_ref, bd_ref, tri_ref, msk_ref) = refs[:18]
    refs = refs[18:]
    if has_vres:
        vf_ref, v0_ref, v1_ref, v2_ref = refs[:4]
        refs = refs[4:]
        y_ref, st_s, cb_s = refs
    else:
        y_ref, vfo_ref, st_s, cb_s = refs
    C, T = RK_CHUNK, RK_TILE

    @pl.when(pl.program_id(0) == 0)
    def _():
        st_s[...] = jnp.zeros_like(st_s)
        cb_s[...] = jnp.zeros_like(cb_s)

    def shift_mix(idx, ref):
        c = ref[...]
        cb_s[idx, 8:8 + T, :] = c
        cp = cb_s[idx, 7:7 + T, :]
        cb_s[idx, 0:8, :] = c[T - 8:T, :]
        return c + (cp - c) * mu_ref[idx:idx + 1, :]

    r = shift_mix(0, zr_ref)
    k0 = shift_mix(1, zk_ref)
    v = shift_mix(2, zv_ref)
    lo = shift_mix(3, zl_ref)

    bd = bd_ref[...]
    wpre = w0_ref[...] + _mm(jnp.tanh(lo), w2_ref[...])
    sp = jnp.maximum(-wpre, 0.0) + jnp.log(1.0 + jnp.exp(-jnp.abs(wpre)))
    logw = -jnp.exp(-sp - 0.5)
    a = _sigmoid(a0_ref[...] + _mm(lo, a2_ref[...]))
    g = _mm(_sigmoid(lo), g2_ref[...])
    if has_vres:
        mix = _sigmoid(v0_ref[...] + _mm(_mm(v, v1_ref[...]), v2_ref[...]))
        v = v + (vf_ref[...] - v) * mix
    else:
        vfo_ref[...] = v
    kk = k0 * kkw_ref[...]
    kk = kk / jnp.maximum(jnp.sqrt(_sel_right(kk * kk, bd)), 1e-12)
    k = k0 * (1.0 + (a - 1.0) * kaw_ref[...])
    beta = kk * a

    lw = _sel_left(tri_ref[...], logw)
    e_neg = jnp.exp(-lw)
    kap = kk * jnp.exp(lw - logw)
    bh = beta * e_neg
    kh = k * e_neg
    rh = r * jnp.exp(lw)
    bdf = bd.astype(F32)

    def stack(x):
        return jnp.concatenate([x.astype(BF16)] * RK_HEADS, axis=0) * bd

    def packed_t(x):
        full = (jnp.concatenate([x] * RK_HEADS, axis=0) * bdf).T
        out = full[0:C, :]
        for h in range(1, RK_HEADS):
            out = out + full[h * C:(h + 1) * C, :]
        return out

    def mm(x, w, dims=NN):
        return _dot(x.astype(BF16), w, dims)

    chunks = range(T // C)
    rows = [slice(c * C, (c + 1) * C) for c in chunks]
    each = lambda fn, *lists: [fn(*args) for args in zip(*lists)]
    lwc = [lw[(c + 1) * C - 1:(c + 1) * C, :] for c in chunks]
    e_end = [jnp.exp(lwc[c] - lw[rows[c], :]) for c in chunks]
    kap_c = [kap[rw, :] for rw in rows]
    rh_c = [rh[rw, :] for rw in rows]
    bh4 = [stack(bh[rw, :]) for rw in rows]
    kh4 = [stack(kh[rw, :]) for rw in rows]
    v4 = [stack(v[rw, :]) for rw in rows]
    g_kb = each(lambda x, w: mm(x, w, NT), kap_c, bh4)
    a_k = each(lambda x, w: mm(x, w, NT) * msk_ref[M_STRICT], kap_c, kh4)
    a_rk = each(lambda x, w: mm(x, w, NT) * msk_ref[M_LOWER], rh_c, kh4)
    a_rb = each(lambda x, w: mm(x, w, NT) * msk_ref[M_LOWER], rh_c, bh4)
    x = [msk_ref[M_EYE] - gc * msk_ref[M_LEVEL0] for gc in g_kb]
    for lv in range(1, RK_LEVELS):
        xl = each(lambda xc, gc: mm(xc, stack(gc * msk_ref[M_LEVEL0 + lv])), x, g_kb)
        x = each(lambda xc, xlc: xc - mm(xlc, stack(xc)), x, xl)
    m1 = each(lambda xc, kc: mm(xc, stack(kc)), x, kap_c)
    akv = each(mm, a_k, v4)
    m2 = each(lambda xc, ac: mm(xc, stack(ac)), x, akv)
    m14 = [stack(t) for t in m1]
    m24 = [stack(t) for t in m2]
    q1 = each(lambda rc, ac, mc: rc - mm(ac, mc), rh_c, a_rb, m14)
    q2 = each(lambda ark, vc, arb, mc: mm(ark, vc) - mm(arb, mc), a_rk, v4, a_rb, m24)
    bwt = [packed_t(beta[rows[c], :] * e_end[c]) for c in chunks]
    kwt = [packed_t(k[rows[c], :] * e_end[c]) for c in chunks]
    pt = each(lambda lc, bc, mc: msk_ref[M_EYE] * jnp.exp(lc) - mm(bc, mc), lwc, bwt, m14)
    rt = each(lambda kc, vc, bc, mc: mm(kc, vc) - mm(bc, mc), kwt, v4, bwt, m24)

    st = st_s[...]
    ys = []
    for c in chunks:
        st4 = stack(st)
        ys.append(mm(q1[c], st4) + q2[c])
        st = mm(pt[c], st4) + rt[c]
    st_s[...] = st
    y = jnp.concatenate(ys, axis=0)

    inv_n = 1.0 / RK_HDIM
    mean = _sel_right(y, bd) * inv_n
    d = y - mean
    var = _sel_right(d * d, bd) * inv_n
    yn = d * lax.rsqrt(var + RK_GN_EPS) * lnw_ref[...] + lnb_ref[...]
    bonus = _sel_right(r * k * rkw_ref[...], bd) * v
    y_ref[...] = (yn + bonus) * g


def _rwkv_params(w, l):
    row = lambda a: a.reshape(1, -1)
    n_dec, n_a = w["rwkv_w2"].shape[1], w["rwkv_a2"].shape[1]
    prm = {
        "mu": w["rwkv_mu"][l].reshape(4, RK_WIDTH),
        "w0": row(w["rwkv_w0"][l]), "a0": row(w["rwkv_a0"][l]),
        "w2": _pad_rows(w["rwkv_w2"][l], 0, RK_WIDTH).astype(BF16),
        "a2": _pad_rows(w["rwkv_a2"][l], n_dec, RK_WIDTH).astype(BF16),
        "g2": _pad_rows(w["rwkv_g2"][l], n_dec + n_a, RK_WIDTH).astype(BF16),
        "k_k": row(w["rwkv_k_k"][l]), "k_a": row(w["rwkv_k_a"][l]), "r_k": row(w["rwkv_r_k"][l]),
        "ln_w": row(w["rwkv_ln_w"][l]), "ln_b": row(w["rwkv_ln_b"][l]),
    }
    if l > 0:
        v1, v2 = w["rwkv_v1"][l - 1], w["rwkv_v2"][l - 1]
        prm["v0"] = row(w["rwkv_v0"][l - 1])
        prm["v1"] = jnp.pad(v1, ((0, 0), (0, LANES - v1.shape[1]))).astype(BF16)
        prm["v2"] = _pad_rows(v2, 0, LANES).astype(BF16)
    return prm


def _rwkv(z, prm, v_first):
    has_vres = v_first is not None
    C, T = RK_CHUNK, RK_TILE
    assert RK_HEADS * C == MXU_DIM and C == RK_HDIM and (1 << RK_LEVELS) == C
    ccol = (HG_HEADS * 4 * HG_DIM + 3 * SW_WIDTH) // RK_WIDTH
    zcol = lambda off: pl.BlockSpec((T, RK_WIDTH), lambda i: (i, ccol + off))
    row = pl.BlockSpec((T, RK_WIDTH), lambda i: (i, 0))
    lane = np.arange(RK_WIDTH)
    bd = jnp.asarray((lane[:, None] // RK_HDIM) == (lane[None, :] // RK_HDIM), BF16)
    t = np.arange(T)
    tri = jnp.asarray(((t[:, None] // C) == (t[None, :] // C)) & (t[None, :] <= t[:, None]), BF16)
    args = [z, z, z, z, prm["mu"], prm["w0"], prm["a0"], prm["w2"], prm["a2"], prm["g2"],
            prm["k_k"], prm["k_a"], prm["r_k"], prm["ln_w"], prm["ln_b"], bd, tri,
            jnp.asarray(_rwkv_masks())]
    in_specs = [zcol(0), zcol(1), zcol(2), zcol(3)] + [_vmem_spec()] * 14
    if has_vres:
        args += [v_first, prm["v0"], prm["v1"], prm["v2"]]
        in_specs += [row] + [_vmem_spec()] * 3
        out_specs = row
        out_shape = jax.ShapeDtypeStruct((SEQ, RK_WIDTH), F32)
    else:
        out_specs = [row, row]
        out_shape = [jax.ShapeDtypeStruct((SEQ, RK_WIDTH), F32)] * 2
    out = pl.pallas_call(
        functools.partial(_rwkv_kernel, has_vres=has_vres),
        grid=(SEQ // T,),
        in_specs=in_specs,
        out_specs=out_specs,
        out_shape=out_shape,
        scratch_shapes=[pltpu.VMEM((RK_HDIM, RK_WIDTH), F32), pltpu.VMEM((4, T + 8, RK_WIDTH), F32)],
        compiler_params=_params(("arbitrary",)),
        name="rwkv7",
    )(*args)
    if has_vres:
        return out, v_first
    return out[0], out[1]


def _post_kernel(h_ref, oa_ref, ob_ref, oc_ref, p_ref, wout_ref, gffn_ref, wup_ref, cw_ref, cb_ref,
                 wdn_ref, gple_ref, wg_ref, wp_ref, gfin_ref, out_ref,
                 hn_s, carry_s, ubuf_s, *, final):
    T = ROW_TILE

    @pl.when(pl.program_id(0) == 0)
    def _():
        carry_s[...] = jnp.zeros_like(carry_s)

    mix = (_dot(oa_ref[...].astype(BF16), wout_ref[0:HG_WIDTH, :])
           + _dot(ob_ref[...].astype(BF16), wout_ref[HG_WIDTH:HG_WIDTH + SW_WIDTH, :])
           + _dot(oc_ref[...].astype(BF16), wout_ref[HG_WIDTH + SW_WIDTH:, :]))
    h1 = h_ref[...] + mix
    hn_s[...] = _rms(h1, gffn_ref[...]).astype(BF16)

    def up(slot, col):
        cols = pl.ds(col, FFN_COLS)
        u = _dot(hn_s[...], wup_ref[:, cols])
        ubuf_s[slot, 0:8, :] = carry_s[:, cols]
        ubuf_s[slot, 8:8 + T, :] = u
        carry_s[:, cols] = u[T - 8:T, :]
        return u

    def conv(slot, col, u):
        cols = pl.ds(col, FFN_COLS)
        return (cb_ref[:, cols] + cw_ref[0:1, cols] * ubuf_s[slot, 6:6 + T, :]
                + cw_ref[1:2, cols] * ubuf_s[slot, 7:7 + T, :] + cw_ref[2:3, cols] * u)

    def up_pair(j):
        return up(2 * (j % 2), j * FFN_COLS), up(2 * (j % 2) + 1, j * FFN_COLS + D_FF)

    n_steps = D_FF // FFN_COLS
    acc = h1
    u_gate, u_val = up_pair(0)
    for j in range(n_steps):
        nxt = up_pair(j + 1) if j + 1 < n_steps else None
        gate = conv(2 * (j % 2), j * FFN_COLS, u_gate)
        val = conv(2 * (j % 2) + 1, j * FFN_COLS + D_FF, u_val)
        act = (_silu(gate) * val).astype(BF16)
        acc = acc + _dot(act, wdn_ref[pl.ds(j * FFN_COLS, FFN_COLS), :])
        if nxt is not None:
            u_gate, u_val = nxt

    h2 = acc
    gate = _sigmoid(_dot(_rms(h2, gple_ref[...]).astype(BF16), wg_ref[...]))
    pe = _dot(p_ref[...].astype(BF16), wp_ref[...])
    h3 = h2 + gate * pe
    out_ref[...] = _rms(h3, gfin_ref[...]) if final else h3


def _post(h, o_a, o_b, o_c, p_l, prm, final):
    T = ROW_TILE
    tile = lambda w: pl.BlockSpec((T, w), lambda i: (i, 0))
    return pl.pallas_call(
        functools.partial(_post_kernel, final=final),
        grid=(SEQ // T,),
        in_specs=[tile(D_MODEL), tile(HG_WIDTH), tile(SW_WIDTH), tile(RK_WIDTH), tile(PLE_DIM)]
                 + [_vmem_spec()] * 10,
        out_specs=tile(D_MODEL),
        out_shape=jax.ShapeDtypeStruct((SEQ, D_MODEL), F32),
        scratch_shapes=[pltpu.VMEM((T, D_MODEL), BF16),
                        pltpu.VMEM((8, 2 * D_FF), F32), pltpu.VMEM((4, T + 8, FFN_COLS), F32)],
        compiler_params=_params(("arbitrary",)),
        name="outproj_ffn_ple",
    )(h, o_a, o_b, o_c, p_l, prm["w_out"], prm["g_ffn"], prm["w_up"], prm["conv_w"], prm["conv_b"],
      prm["w_down"], prm["g_ple"], prm["w_gate"], prm["w_ple"], prm["g_final"])


def _pad_rows(w, start, total):
    out = jnp.zeros((total, w.shape[1]), w.dtype)
    return lax.dynamic_update_slice(out, w, (start, 0))


def kernel(x, p, w_in, w_out, norm_mix_g, norm_ffn_g, norm_ple_g, final_norm_g, hgrn_lower_bounds, hgrn_gnorm_g, rwkv_mu, rwkv_w0, rwkv_w2, rwkv_a0, rwkv_a2, rwkv_g2, rwkv_k_k, rwkv_k_a, rwkv_r_k, rwkv_ln_w, rwkv_ln_b, rwkv_v0, rwkv_v1, rwkv_v2, ffn_up, ffn_conv_w, ffn_conv_b, ffn_down, ple_proj, ple_gate):
    row = lambda a: a.reshape(1, -1)
    h = x.reshape(SEQ, D_MODEL)
    v_first = None
    rwkv_w = dict(rwkv_mu=rwkv_mu, rwkv_w0=rwkv_w0, rwkv_w2=rwkv_w2, rwkv_a0=rwkv_a0, rwkv_a2=rwkv_a2,
                  rwkv_g2=rwkv_g2, rwkv_k_k=rwkv_k_k, rwkv_k_a=rwkv_k_a, rwkv_r_k=rwkv_r_k,
                  rwkv_ln_w=rwkv_ln_w, rwkv_ln_b=rwkv_ln_b, rwkv_v0=rwkv_v0, rwkv_v1=rwkv_v1, rwkv_v2=rwkv_v2)
    for l in range(DEPTH):
        z = _inproj(h, row(norm_mix_g[l]), w_in[l].astype(BF16))
        o_a = _hgrn(z, hgrn_lower_bounds, row(hgrn_gnorm_g[l]), l)
        o_b = _dilated_mix(z)
        o_c, v_first = _rwkv(z, _rwkv_params(rwkv_w, l), v_first)
        post = {
            "w_out": w_out[l].astype(BF16), "g_ffn": row(norm_ffn_g[l]),
            "w_up": ffn_up[l].astype(BF16), "conv_w": ffn_conv_w[l], "conv_b": row(ffn_conv_b[l]),
            "w_down": ffn_down[l].astype(BF16), "g_ple": row(norm_ple_g[l]),
            "w_gate": ple_gate[l].astype(BF16), "w_ple": ple_proj[l].astype(BF16),
            "g_final": row(final_norm_g),
        }
        h = _post(h, o_a, o_b, o_c, p[l].reshape(SEQ, PLE_DIM), post, final=l == DEPTH - 1)
    return h.reshape(1, SEQ, D_MODEL)
```

```python
import functools

import numpy as np
import jax
import jax.numpy as jnp
from jax import lax
from jax.experimental import pallas as pl
from jax.experimental.pallas import tpu as pltpu

F32 = jnp.float32
BF16 = jnp.bfloat16

D_MODEL = 1024
SEQ = 16384
DEPTH = 2
HG_HEADS = 4
HG_DIM = 128
HG_WIDTH = 512
SW_HEADS = 4
SW_HDIM = 64
SW_WIDTH = 256
SW_PATTERNS = ((128, 1), (512, 4), (2048, 16))
SW_BLOCK = 128
RK_HEADS = 4
RK_HDIM = 64
RK_WIDTH = 256
RK_GN_EPS = 64e-5
N_IN = 3840
D_FF = 2816
PLE_DIM = 256
NORM_EPS = 1e-6

LANES = 128
MXU_DIM = 256
VMEM_LIMIT_BYTES = 56 * 1024 * 1024

ROW_TILE = 512
FFN_COLS = 256
HG_TILE = 256
HG_CHUNK = 64
HG_SAFE_EXP = 80.0
AT_SPAN = SW_BLOCK * max(d for _, d in SW_PATTERNS)
ATTN_MASKED = -1e30
AT_GROUP = 2
RK_CHUNK = 64
RK_TILE = 512
RK_LEVELS = 6

NN = (((1,), (0,)), ((), ()))
NT = (((1,), (1,)), ((), ()))


def _dot(a, b, dims=NN):
    return lax.dot_general(a, b, dims, preferred_element_type=F32)


def _mm(a, b, dims=NN):
    return _dot(a.astype(BF16), b.astype(BF16), dims)


def _split3(x):
    x1 = x.astype(BF16)
    r1 = x - x1.astype(F32)
    x2 = r1.astype(BF16)
    x3 = (r1 - x2.astype(F32)).astype(BF16)
    return x1, x2, x3


def _sel_left(m01, x, terms=3):
    x1, x2, x3 = _split3(x)
    out = _dot(m01, x1) + _dot(m01, x2)
    return out + _dot(m01, x3) if terms == 3 else out


def _sel_right(x, m01):
    x1, x2, x3 = _split3(x)
    return _dot(x1, m01) + (_dot(x2, m01) + _dot(x3, m01))


def _sigmoid(x):
    return 1.0 / (1.0 + jnp.exp(-x))


def _silu(x):
    return x * _sigmoid(x)


def _rms(x, g):
    return x * lax.rsqrt(jnp.mean(x * x, axis=-1, keepdims=True) + NORM_EPS) * g


def _vmem_spec():
    return pl.BlockSpec(memory_space=pltpu.VMEM)


def _params(sem):
    return pltpu.CompilerParams(dimension_semantics=sem, vmem_limit_bytes=VMEM_LIMIT_BYTES)


def _inproj_kernel(h_ref, g_ref, w_ref, z_ref):
    y = _rms(h_ref[...], g_ref[...])
    z_ref[...] = _dot(y.astype(BF16), w_ref[...])


def _inproj(h, g, w_bf):
    return pl.pallas_call(
        _inproj_kernel,
        grid=(SEQ // ROW_TILE,),
        in_specs=[pl.BlockSpec((ROW_TILE, D_MODEL), lambda i: (i, 0)),
                  _vmem_spec(), _vmem_spec()],
        out_specs=pl.BlockSpec((ROW_TILE, N_IN), lambda i: (i, 0)),
        out_shape=jax.ShapeDtypeStruct((SEQ, N_IN), F32),
        compiler_params=_params(("arbitrary",)),
        name="inproj",
    )(h, g, w_bf)


def _hgrn_kernel(q_ref, f_ref, i_ref, g_ref, lbp_ref, gn_ref, tri_ref, o_ref,
                 st_s, b_s, qq_s, kk_s, oi_s, *, layer):
    C, T, D = HG_CHUNK, HG_TILE, HG_DIM

    @pl.when(pl.program_id(0) == 0)
    def _():
        st_s[...] = jnp.zeros_like(st_s)

    lbp = lbp_ref[...]
    e = jnp.exp(lbp - jnp.max(lbp, axis=0, keepdims=True))
    sm = e / jnp.sum(e, axis=0, keepdims=True)
    lb = jnp.zeros((1, HG_WIDTH), F32)
    for j in range(1, layer + 1):
        lb = lb + sm[j:j + 1, :]

    q = _silu(q_ref[...])
    f = lb + (1.0 - lb) * _sigmoid(f_ref[...])
    kk = 1.0 - f
    lf = jnp.log(f)
    b = _sel_left(tri_ref[...], lf, terms=2)
    v = i_ref[...]
    b_s[...] = b
    qq_s[...] = q
    kk_s[...] = kk
    safe = jnp.min(b) > -HG_SAFE_EXP
    qe = (q * jnp.exp(b)).astype(BF16)
    kt = (kk * jnp.exp(-jnp.maximum(b, -HG_SAFE_EXP))).astype(BF16)
    vb = v.astype(BF16)
    half = lax.broadcasted_iota(jnp.int32, (2 * C, D), 0) < C
    causal = (lax.broadcasted_iota(jnp.int32, (C, C), 1) <= lax.broadcasted_iota(jnp.int32, (C, C), 0))

    heads = [slice(h * D, (h + 1) * D) for h in range(HG_HEADS)]
    units = [(n, h) for n in range(T // C) for h in range(HG_HEADS)]
    rows = [slice(n * C, (n + 1) * C) for n in range(T // C)]
    vts = [v[:, lanes].T.astype(BF16) for lanes in heads]
    tot = {(n, h): b[(n + 1) * C - 1:(n + 1) * C, heads[h]] for n, h in units}
    scores = {(n, h): _dot(qe[rows[n], heads[h]], kt[rows[n], heads[h]], NT) for n, h in units}
    intra = {(n, h): _dot(jnp.where(causal, scores[n, h], 0.0).astype(BF16), vb[rows[n], heads[h]])
             for n, h in units}
    upd = {}
    for n, h in units:
        slab = slice((n // 2) * 2 * C, (n // 2 + 1) * 2 * C)
        first = half if n % 2 == 0 else jnp.logical_not(half)
        ke = jnp.where(first, kk[slab, heads[h]] * jnp.exp(tot[n, h] - b[slab, heads[h]]), 0.0)
        upd[n, h] = _dot(vts[h][:, slab], ke.astype(BF16))
    state = {(0, h): st_s[h] for h in range(HG_HEADS)}
    for n, h in units:
        state[n + 1, h] = state[n, h] * jnp.exp(tot[n, h]) + upd[n, h]
    for h in range(HG_HEADS):
        st_s[h] = state[T // C, h]
    for n, h in units:
        inter = _dot(qe[rows[n], heads[h]], state[n, h].astype(BF16), NT)
        oi_s[rows[n], heads[h]] = inter
        o_ref[rows[n], heads[h]] = inter + intra[n, h]

    @pl.when(jnp.logical_not(safe))
    def _():
        rid = lax.broadcasted_iota(jnp.int32, (C, D), 0)

        def pick(x, i):
            return jnp.sum(jnp.where(rid == i, x, 0.0), axis=0, keepdims=True)

        for h in range(HG_HEADS):
            lanes = slice(h * D, (h + 1) * D)
            for n in range(T // C):
                rows = slice(n * C, (n + 1) * C)
                b_n, q_n, k_n, v_n = b_s[rows, lanes], qq_s[rows, lanes], kk_s[rows, lanes], i_ref[rows, lanes]

                def row(i, od, b_n=b_n, q_n=q_n, k_n=k_n, v_n=v_n):
                    w = jnp.exp(pick(b_n, i) - b_n) * (pick(q_n, i) * k_n)
                    s = jnp.sum(jnp.where(rid <= i, w, 0.0), axis=-1, keepdims=True)
                    return jnp.where(rid == i, jnp.sum(s * v_n, axis=0, keepdims=True), od)

                o_ref[rows, lanes] = oi_s[rows, lanes] + lax.fori_loop(0, C, row, jnp.zeros((C, D), F32))

    gate = gn_ref[...] * _silu(g_ref[...])
    for h in range(HG_HEADS):
        lanes = slice(h * D, (h + 1) * D)
        o = o_ref[:, lanes]
        o = o * lax.rsqrt(jnp.mean(o * o, axis=-1, keepdims=True) + NORM_EPS)
        o_ref[:, lanes] = o * gate[:, lanes]


def _hgrn(z, lb_params, gnorm_g, layer):
    T, C = HG_TILE, HG_CHUNK
    assert HG_DIM == LANES and T % (2 * C) == 0
    t = np.arange(T)
    tri = jnp.asarray(((t[:, None] // C) == (t[None, :] // C)) & (t[None, :] <= t[:, None]), BF16)
    col = lambda j: pl.BlockSpec((T, HG_WIDTH), lambda i: (i, j))
    scr = pltpu.VMEM((T, HG_WIDTH), F32)
    return pl.pallas_call(
        functools.partial(_hgrn_kernel, layer=layer),
        grid=(SEQ // T,),
        in_specs=[col(0), col(1), col(2), col(3), _vmem_spec(), _vmem_spec(), _vmem_spec()],
        out_specs=pl.BlockSpec((T, HG_WIDTH), lambda i: (i, 0)),
        out_shape=jax.ShapeDtypeStruct((SEQ, HG_WIDTH), F32),
        scratch_shapes=[pltpu.VMEM((HG_HEADS, HG_DIM, HG_DIM), F32), scr, scr, scr, scr],
        compiler_params=_params(("arbitrary",)),
        name="hgrn2",
    )(z, z, z, z, lb_params, gnorm_g, tri)


def _attn_kernel(*refs):
    nh = SW_WIDTH // LANES
    q_ref, ko_ref, vo_ref, kp_ref, vp_ref = (refs[j * nh:(j + 1) * nh] for j in range(5))
    hm_ref, o_ref = refs[5 * nh], refs[5 * nh + 1]
    acc_s, m_s, l_s = (refs[5 * nh + 2 + j * nh:5 * nh + 2 + (j + 1) * nh] for j in range(3))

    def ld(halves, rows):
        return jnp.concatenate([h[rows, :] for h in halves], axis=1)

    def st(halves, rows, val):
        for j, h in enumerate(halves):
            h[rows, :] = val[:, j * LANES:(j + 1) * LANES]

    B = SW_BLOCK
    i = pl.program_id(0)
    hm = hm_ref[...]
    shape = (SW_HEADS * B, 2 * B)
    qi = lax.broadcasted_iota(jnp.int32, shape, 0) & (B - 1)
    kj = lax.broadcasted_iota(jnp.int32, shape, 1)
    band = (kj >= qi) & (kj <= qi + B)
    own_half = kj >= B
    lane_head = lax.broadcasted_iota(jnp.int32, (B, SW_WIDTH), 1) // SW_HDIM

    for pidx, (window, dil) in enumerate(SW_PATTERNS):
        per_res = AT_SPAN // (dil * B)

        def rows_at(s, dil=dil):
            if dil == 1:
                return pl.ds(pl.multiple_of(s, B), B)
            return pl.ds(s, B, stride=dil)

        def scores(u, dil=dil, per_res=per_res):
            r = u // per_res
            mb = u % per_res
            start = r + dil * B * mb
            rows = rows_at(start)
            q = ld(q_ref, rows) * (SW_HDIM ** -0.5)
            prow = pl.ds(AT_SPAN - dil * B + r, B, stride=dil) if dil > 1 else pl.ds(AT_SPAN - B, B)
            k_prev = ld(kp_ref, prow)
            v_prev = ld(vp_ref, prow)
            if per_res > 1:
                inner = mb > 0
                srows = rows_at(jnp.where(inner, start - dil * B, start))
                k_prev = jnp.where(inner, ld(ko_ref, srows), k_prev)
                v_prev = jnp.where(inner, ld(vo_ref, srows), v_prev)
                valid = band & (own_half | inner | (i > 0))
            else:
                valid = band & (own_half | (i > 0))
            kw = jnp.concatenate([k_prev, ld(ko_ref, rows)], axis=0).astype(BF16)
            vw = jnp.concatenate([v_prev, ld(vo_ref, rows)], axis=0).astype(BF16)
            q4 = (jnp.concatenate([q] * SW_HEADS, axis=0) * hm).astype(BF16)
            return rows, jnp.where(valid, _dot(q4, kw, NT), ATTN_MASKED), vw

        def softmax(s):
            m = jnp.max(s, axis=-1, keepdims=True)
            pe = jnp.exp(s - m)
            return m, pe.astype(BF16), jnp.sum(pe, axis=-1, keepdims=True)

        def merge(rows, m, l, pv, pidx=pidx):
            acc = pv[0:B, :]
            mf = jnp.broadcast_to(m[0:B, :], (B, SW_WIDTH))
            lf = jnp.broadcast_to(l[0:B, :], (B, SW_WIDTH))
            for h in range(1, SW_HEADS):
                hs = slice(h * B, (h + 1) * B)
                sel = lane_head == h
                acc = jnp.where(sel, pv[hs, :], acc)
                mf = jnp.where(sel, m[hs, :], mf)
                lf = jnp.where(sel, l[hs, :], lf)
            if pidx > 0:
                m_old = ld(m_s, rows)
                m_new = jnp.maximum(m_old, mf)
                a_old = jnp.exp(m_old - m_new)
                a_cur = jnp.exp(mf - m_new)
                acc = ld(acc_s, rows) * a_old + acc * a_cur
                lf = ld(l_s, rows) * a_old + lf * a_cur
                mf = m_new
            return acc, mf, lf

        def group(gi, carry):
            units = [gi * AT_GROUP + t for t in range(AT_GROUP)]
            sc = [scores(u) for u in units]
            sm = [softmax(s) for _, s, _ in sc]
            pv = [_dot(pe, vw) for (_, pe, _), (_, _, vw) in zip(sm, sc)]
            out = [merge(rows, m, l, p) for (rows, _, _), (m, _, l), p in zip(sc, sm, pv)]
            for (rows, _, _), (acc, mf, lf) in zip(sc, out):
                st(acc_s, rows, acc)
                st(m_s, rows, mf)
                st(l_s, rows, lf)
            return carry

        lax.fori_loop(0, AT_SPAN // (B * AT_GROUP), group, 0)

    for j in range(nh):
        o_ref[:, j * LANES:(j + 1) * LANES] = acc_s[j][...] / l_s[j][...]


def _dilated_mix(z):
    for window, dil in SW_PATTERNS:
        assert window // dil == SW_BLOCK and AT_SPAN % (dil * SW_BLOCK) == 0
    nh = SW_WIDTH // LANES
    qcol = (HG_HEADS * 4 * HG_DIM) // LANES
    own = lambda c: pl.BlockSpec((AT_SPAN, LANES), lambda i: (i, qcol + c))
    prev = lambda c: pl.BlockSpec((AT_SPAN, LANES), lambda i: (jnp.maximum(i - 1, 0), qcol + c))
    row_head = np.arange(SW_HEADS * SW_BLOCK) // SW_BLOCK
    lane_head = np.arange(SW_WIDTH) // SW_HDIM
    hm = jnp.asarray(row_head[:, None] == lane_head[None, :], F32)
    scr = pltpu.VMEM((AT_SPAN, LANES), F32)
    return pl.pallas_call(
        _attn_kernel,
        grid=(SEQ // AT_SPAN,),
        in_specs=[own(c) for c in range(3 * nh)] + [prev(c) for c in range(nh, 3 * nh)] + [_vmem_spec()],
        out_specs=pl.BlockSpec((AT_SPAN, SW_WIDTH), lambda i: (i, 0)),
        out_shape=jax.ShapeDtypeStruct((SEQ, SW_WIDTH), F32),
        scratch_shapes=[scr] * (3 * nh),
        compiler_params=_params(("arbitrary",)),
        name="dilated_attn",
    )(*([z] * (5 * nh)), hm)


def _rwkv_masks():
    t = np.arange(RK_CHUNK)[:, None]
    s = np.tile(np.arange(RK_CHUNK), RK_HEADS)[None, :]
    masks = [s < t, s <= t, s == t]
    for lv in range(RK_LEVELS):
        masks.append(((t >> (lv + 1)) == (s >> (lv + 1))) & (((t >> lv) & 1) == 1) & (((s >> lv) & 1) == 0))
    return np.stack(masks).astype(np.float32)


M_STRICT, M_LOWER, M_EYE, M_LEVEL0 = 0, 1, 2, 3


def _rwkv_kernel(*refs, has_vres):
    (zr_ref, zk_ref, zv_ref, zl_ref, mu_ref, w0_ref, a0_ref, w2_ref, a2_ref, g2_ref,
     kkw_ref, kaw_ref, rkw_ref, lnw_ref, lnb_ref, bd_ref, tri_ref, msk_ref) = refs[:18]
    refs = refs[18:]
    if has_vres:
        vf_ref, v0_ref, v1_ref, v2_ref = refs[:4]
        refs = refs[4:]
        y_ref, st_s, cb_s = refs
    else:
        y_ref, vfo_ref, st_s, cb_s = refs
    C, T = RK_CHUNK, RK_TILE

    @pl.when(pl.program_id(0) == 0)
    def _():
        st_s[...] = jnp.zeros_like(st_s)
        cb_s[...] = jnp.zeros_like(cb_s)

    def shift_mix(idx, ref):
        c = ref[...]
        cb_s[idx, 8:8 + T, :] = c
        cp = cb_s[idx, 7:7 + T, :]
        cb_s[idx, 0:8, :] = c[T - 8:T, :]
        return c + (cp - c) * mu_ref[idx:idx + 1, :]

    r = shift_mix(0, zr_ref)
    k0 = shift_mix(1, zk_ref)
    v = shift_mix(2, zv_ref)
    lo = shift_mix(3, zl_ref)

    bd = bd_ref[...]
    wpre = w0_ref[...] + _mm(jnp.tanh(lo), w2_ref[...])
    sp = jnp.maximum(-wpre, 0.0) + jnp.log(1.0 + jnp.exp(-jnp.abs(wpre)))
    logw = -jnp.exp(-sp - 0.5)
    a = _sigmoid(a0_ref[...] + _mm(lo, a2_ref[...]))
    g = _mm(_sigmoid(lo), g2_ref[...])
    if has_vres:
        mix = _sigmoid(v0_ref[...] + _mm(_mm(v, v1_ref[...]), v2_ref[...]))
        v = v + (vf_ref[...] - v) * mix
    else:
        vfo_ref[...] = v
    kk = k0 * kkw_ref[...]
    kk = kk / jnp.maximum(jnp.sqrt(_sel_right(kk * kk, bd)), 1e-12)
    k = k0 * (1.0 + (a - 1.0) * kaw_ref[...])
    beta = kk * a

    lw = _sel_left(tri_ref[...], logw)
    e_neg = jnp.exp(-lw)
    kap = kk * jnp.exp(lw - logw)
    bh = beta * e_neg
    kh = k * e_neg
    rh = r * jnp.exp(lw)
    bdf = bd.astype(F32)

    def stack(x):
        return jnp.concatenate([x.astype(BF16)] * RK_HEADS, axis=0) * bd

    def packed_t(x):
        full = (jnp.concatenate([x] * RK_HEADS, axis=0) * bdf).T
        out = full[0:C, :]
        for h in range(1, RK_HEADS):
            out = out + full[h * C:(h + 1) * C, :]
        return out

    def mm(x, w, dims=NN):
        return _dot(x.astype(BF16), w, dims)

    chunks = range(T // C)
    rows = [slice(c * C, (c + 1) * C) for c in chunks]
    each = lambda fn, *lists: [fn(*args) for args in zip(*lists)]
    lwc = [lw[(c + 1) * C - 1:(c + 1) * C, :] for c in chunks]
    e_end = [jnp.exp(lwc[c] - lw[rows[c], :]) for c in chunks]
    kap_c = [kap[rw, :] for rw in rows]
    rh_c = [rh[rw, :] for rw in rows]
    bh4 = [stack(bh[rw, :]) for rw in rows]
    kh4 = [stack(kh[rw, :]) for rw in rows]
    v4 = [stack(v[rw, :]) for rw in rows]
    g_kb = each(lambda x, w: mm(x, w, NT), kap_c, bh4)
    a_k = each(lambda x, w: mm(x, w, NT) * msk_ref[M_STRICT], kap_c, kh4)
    a_rk = each(lambda x, w: mm(x, w, NT) * msk_ref[M_LOWER], rh_c, kh4)
    a_rb = each(lambda x, w: mm(x, w, NT) * msk_ref[M_LOWER], rh_c, bh4)
    x = [msk_ref[M_EYE] - gc * msk_ref[M_LEVEL0] for gc in g_kb]
    for lv in range(1, RK_LEVELS):
        xl = each(lambda xc, gc: mm(xc, stack(gc * msk_ref[M_LEVEL0 + lv])), x, g_kb)
        x = each(lambda xc, xlc: xc - mm(xlc, stack(xc)), x, xl)
    m1 = each(lambda xc, kc: mm(xc, stack(kc)), x, kap_c)
    akv = each(mm, a_k, v4)
    m2 = each(lambda xc, ac: mm(xc, stack(ac)), x, akv)
    m14 = [stack(t) for t in m1]
    m24 = [stack(t) for t in m2]
    q1 = each(lambda rc, ac, mc: rc - mm(ac, mc), rh_c, a_rb, m14)
    q2 = each(lambda ark, vc, arb, mc: mm(ark, vc) - mm(arb, mc), a_rk, v4, a_rb, m24)
    bwt = [packed_t(beta[rows[c], :] * e_end[c]) for c in chunks]
    kwt = [packed_t(k[rows[c], :] * e_end[c]) for c in chunks]
    pt = each(lambda lc, bc, mc: msk_ref[M_EYE] * jnp.exp(lc) - mm(bc, mc), lwc, bwt, m14)
    rt = each(lambda kc, vc, bc, mc: mm(kc, vc) - mm(bc, mc), kwt, v4, bwt, m24)

    st = st_s[...]
    ys = []
    for c in chunks:
        st4 = stack(st)
        ys.append(mm(q1[c], st4) + q2[c])
        st = mm(pt[c], st4) + rt[c]
    st_s[...] = st
    y = jnp.concatenate(ys, axis=0)

    inv_n = 1.0 / RK_HDIM
    mean = _sel_right(y, bd) * inv_n
    d = y - mean
    var = _sel_right(d * d, bd) * inv_n
    yn = d * lax.rsqrt(var + RK_GN_EPS) * lnw_ref[...] + lnb_ref[...]
    bonus = _sel_right(r * k * rkw_ref[...], bd) * v
    y_ref[...] = (yn + bonus) * g


def _rwkv_params(w, l):
    row = lambda a: a.reshape(1, -1)
    n_dec, n_a = w["rwkv_w2"].shape[1], w["rwkv_a2"].shape[1]
    prm = {
        "mu": w["rwkv_mu"][l].reshape(4, RK_WIDTH),
        "w0": row(w["rwkv_w0"][l]), "a0": row(w["rwkv_a0"][l]),
        "w2": _pad_rows(w["rwkv_w2"][l], 0, RK_WIDTH).astype(BF16),
        "a2": _pad_rows(w["rwkv_a2"][l], n_dec, RK_WIDTH).astype(BF16),
        "g2": _pad_rows(w["rwkv_g2"][l], n_dec + n_a, RK_WIDTH).astype(BF16),
        "k_k": row(w["rwkv_k_k"][l]), "k_a": row(w["rwkv_k_a"][l]), "r_k": row(w["rwkv_r_k"][l]),
        "ln_w": row(w["rwkv_ln_w"][l]), "ln_b": row(w["rwkv_ln_b"][l]),
    }
    if l > 0:
        v1, v2 = w["rwkv_v1"][l - 1], w["rwkv_v2"][l - 1]
        prm["v0"] = row(w["rwkv_v0"][l - 1])
        prm["v1"] = jnp.pad(v1, ((0, 0), (0, LANES - v1.shape[1]))).astype(BF16)
        prm["v2"] = _pad_rows(v2, 0, LANES).astype(BF16)
    return prm


def _rwkv(z, prm, v_first):
    has_vres = v_first is not None
    C, T = RK_CHUNK, RK_TILE
    assert RK_HEADS * C == MXU_DIM and C == RK_HDIM and (1 << RK_LEVELS) == C
    ccol = (HG_HEADS * 4 * HG_DIM + 3 * SW_WIDTH) // RK_WIDTH
    zcol = lambda off: pl.BlockSpec((T, RK_WIDTH), lambda i: (i, ccol + off))
    row = pl.BlockSpec((T, RK_WIDTH), lambda i: (i, 0))
    lane = np.arange(RK_WIDTH)
    bd = jnp.asarray((lane[:, None] // RK_HDIM) == (lane[None, :] // RK_HDIM), BF16)
    t = np.arange(T)
    tri = jnp.asarray(((t[:, None] // C) == (t[None, :] // C)) & (t[None, :] <= t[:, None]), BF16)
    args = [z, z, z, z, prm["mu"], prm["w0"], prm["a0"], prm["w2"], prm["a2"], prm["g2"],
            prm["k_k"], prm["k_a"], prm["r_k"], prm["ln_w"], prm["ln_b"], bd, tri,
            jnp.asarray(_rwkv_masks())]
    in_specs = [zcol(0), zcol(1), zcol(2), zcol(3)] + [_vmem_spec()] * 14
    if has_vres:
        args += [v_first, prm["v0"], prm["v1"], prm["v2"]]
        in_specs += [row] + [_vmem_spec()] * 3
        out_specs = row
        out_shape = jax.ShapeDtypeStruct((SEQ, RK_WIDTH), F32)
    else:
        out_specs = [row, row]
        out_shape = [jax.ShapeDtypeStruct((SEQ, RK_WIDTH), F32)] * 2
    out = pl.pallas_call(
        functools.partial(_rwkv_kernel, has_vres=has_vres),
        grid=(SEQ // T,),
        in_specs=in_specs,
        out_specs=out_specs,
        out_shape=out_shape,
        scratch_shapes=[pltpu.VMEM((RK_HDIM, RK_WIDTH), F32), pltpu.VMEM((4, T + 8, RK_WIDTH), F32)],
        compiler_params=_params(("arbitrary",)),
        name="rwkv7",
    )(*args)
    if has_vres:
        return out, v_first
    return out[0], out[1]


def _post_kernel(h_ref, oa_ref, ob_ref, oc_ref, p_ref, wout_ref, gffn_ref, wup_ref, cw_ref, cb_ref,
                 wdn_ref, gple_ref, wg_ref, wp_ref, gfin_ref, out_ref,
                 hn_s, carry_s, act_s, *, final):
    T = ROW_TILE

    @pl.when(pl.program_id(0) == 0)
    def _():
        carry_s[...] = jnp.zeros_like(carry_s)

    mix = (_dot(oa_ref[...].astype(BF16), wout_ref[0:HG_WIDTH, :])
           + _dot(ob_ref[...].astype(BF16), wout_ref[HG_WIDTH:HG_WIDTH + SW_WIDTH, :])
           + _dot(oc_ref[...].astype(BF16), wout_ref[HG_WIDTH + SW_WIDTH:, :]))
    h1 = h_ref[...] + mix
    hn_s[...] = _rms(h1, gffn_ref[...]).astype(BF16)

    rid = lax.broadcasted_iota(jnp.int32, (8, FFN_COLS), 0)

    def up(col):
        return _dot(hn_s[...], wup_ref[:, pl.ds(col, FFN_COLS)])

    def conv(col, u):
        cols = pl.ds(col, FFN_COLS)
        prev = carry_s[:, cols]
        carry_s[:, cols] = u[T - 8:T, :]
        r1 = pltpu.roll(u, 1, 0)
        r2 = pltpu.roll(u, 2, 0)
        top1 = jnp.where(rid == 0, prev[7:8, :], r1[0:8, :])
        top2 = jnp.where(rid == 0, prev[6:7, :], jnp.where(rid == 1, prev[7:8, :], r2[0:8, :]))
        u1 = jnp.concatenate([top1, r1[8:, :]], axis=0)
        u2 = jnp.concatenate([top2, r2[8:, :]], axis=0)
        return cb_ref[:, cols] + cw_ref[0:1, cols] * u2 + cw_ref[1:2, cols] * u1 + cw_ref[2:3, cols] * u

    def up_pair(j):
        return up(j * FFN_COLS), up(j * FFN_COLS + D_FF)

    n_steps = D_FF // FFN_COLS
    u_gate, u_val = up_pair(0)
    for j in range(n_steps):
        nxt = up_pair(j + 1) if j + 1 < n_steps else None
        gate = conv(j * FFN_COLS, u_gate)
        val = conv(j * FFN_COLS + D_FF, u_val)
        act_s[:, pl.ds(j * FFN_COLS, FFN_COLS)] = (_silu(gate) * val).astype(BF16)
        if nxt is not None:
            u_gate, u_val = nxt

    h2 = h1 + _dot(act_s[...], wdn_ref[...])
    gate = _sigmoid(_dot(_rms(h2, gple_ref[...]).astype(BF16), wg_ref[...]))
    pe = _dot(p_ref[...].astype(BF16), wp_ref[...])
    h3 = h2 + gate * pe
    out_ref[...] = _rms(h3, gfin_ref[...]) if final else h3


def _post(h, o_a, o_b, o_c, p_l, prm, final):
    T = ROW_TILE
    tile = lambda w: pl.BlockSpec((T, w), lambda i: (i, 0))
    return pl.pallas_call(
        functools.partial(_post_kernel, final=final),
        grid=(SEQ // T,),
        in_specs=[tile(D_MODEL), tile(HG_WIDTH), tile(SW_WIDTH), tile(RK_WIDTH), tile(PLE_DIM)]
                 + [_vmem_spec()] * 10,
        out_specs=tile(D_MODEL),
        out_shape=jax.ShapeDtypeStruct((SEQ, D_MODEL), F32),
        scratch_shapes=[pltpu.VMEM((T, D_MODEL), BF16),
                        pltpu.VMEM((8, 2 * D_FF), F32), pltpu.VMEM((T, D_FF), BF16)],
        compiler_params=_params(("arbitrary",)),
        name="outproj_ffn_ple",
    )(h, o_a, o_b, o_c, p_l, prm["w_out"], prm["g_ffn"], prm["w_up"], prm["conv_w"], prm["conv_b"],
      prm["w_down"], prm["g_ple"], prm["w_gate"], prm["w_ple"], prm["g_final"])


def _pad_rows(w, start, total):
    out = jnp.zeros((total, w.shape[1]), w.dtype)
    return lax.dynamic_update_slice(out, w, (start, 0))


def kernel(x, p, w_in, w_out, norm_mix_g, norm_ffn_g, norm_ple_g, final_norm_g, hgrn_lower_bounds, hgrn_gnorm_g, rwkv_mu, rwkv_w0, rwkv_w2, rwkv_a0, rwkv_a2, rwkv_g2, rwkv_k_k, rwkv_k_a, rwkv_r_k, rwkv_ln_w, rwkv_ln_b, rwkv_v0, rwkv_v1, rwkv_v2, ffn_up, ffn_conv_w, ffn_conv_b, ffn_down, ple_proj, ple_gate):
    row = lambda a: a.reshape(1, -1)
    h = x.reshape(SEQ, D_MODEL)
    v_first = None
    rwkv_w = dict(rwkv_mu=rwkv_mu, rwkv_w0=rwkv_w0, rwkv_w2=rwkv_w2, rwkv_a0=rwkv_a0, rwkv_a2=rwkv_a2,
                  rwkv_g2=rwkv_g2, rwkv_k_k=rwkv_k_k, rwkv_k_a=rwkv_k_a, rwkv_r_k=rwkv_r_k,
                  rwkv_ln_w=rwkv_ln_w, rwkv_ln_b=rwkv_ln_b, rwkv_v0=rwkv_v0, rwkv_v1=rwkv_v1, rwkv_v2=rwkv_v2)
    for l in range(DEPTH):
        z = _inproj(h, row(norm_mix_g[l]), w_in[l].astype(BF16))
        o_a = _hgrn(z, hgrn_lower_bounds, row(hgrn_gnorm_g[l]), l)
        o_b = _dilated_mix(z)
        o_c, v_first = _rwkv(z, _rwkv_params(rwkv_w, l), v_first)
        post = {
            "w_out": w_out[l].astype(BF16), "g_ffn": row(norm_ffn_g[l]),
            "w_up": ffn_up[l].astype(BF16), "conv_w": ffn_conv_w[l], "conv_b": row(ffn_conv_b[l]),
            "w_down": ffn_down[l].astype(BF16), "g_ple": row(norm_ple_g[l]),
            "w_gate": ple_gate[l].astype(BF16), "w_ple": ple_proj[l].astype(BF16),
            "g_final": row(final_norm_g),
        }
        h = _post(h, o_a, o_b, o_c, p[l].reshape(SEQ, PLE_DIM), post, final=l == DEPTH - 1)
    return h.reshape(1, SEQ, D_MODEL)
```

```python
import functools

import numpy as np
import jax
import jax.numpy as jnp
from jax import lax
from jax.experimental import pallas as pl
from jax.experimental.pallas import tpu as pltpu

F32 = jnp.float32
BF16 = jnp.bfloat16

D_MODEL = 1024
SEQ = 16384
DEPTH = 2
HG_HEADS = 4
HG_DIM = 128
HG_WIDTH = 512
SW_HEADS = 4
SW_HDIM = 64
SW_WIDTH = 256
SW_PATTERNS = ((128, 1), (512, 4), (2048, 16))
SW_BLOCK = 128
RK_HEADS = 4
RK_HDIM = 64
RK_WIDTH = 256
RK_GN_EPS = 64e-5
N_IN = 3840
D_FF = 2816
PLE_DIM = 256
NORM_EPS = 1e-6

LANES = 128
MXU_DIM = 256
VMEM_LIMIT_BYTES = 56 * 1024 * 1024

ROW_TILE = 512
FFN_COLS = 256
HG_TILE = 256
HG_CHUNK = 64
HG_SAFE_EXP = 80.0
AT_SPAN = SW_BLOCK * max(d for _, d in SW_PATTERNS)
ATTN_MASKED = -1e30
AT_GROUP = 2
RK_CHUNK = 64
RK_TILE = 512
RK_LEVELS = 6

NN = (((1,), (0,)), ((), ()))
NT = (((1,), (1,)), ((), ()))


def _dot(a, b, dims=NN):
    return lax.dot_general(a, b, dims, preferred_element_type=F32)


def _mm(a, b, dims=NN):
    return _dot(a.astype(BF16), b.astype(BF16), dims)


def _split3(x):
    x1 = x.astype(BF16)
    r1 = x - x1.astype(F32)
    x2 = r1.astype(BF16)
    x3 = (r1 - x2.astype(F32)).astype(BF16)
    return x1, x2, x3


def _sel_left(m01, x, terms=3):
    x1, x2, x3 = _split3(x)
    out = _dot(m01, x1) + _dot(m01, x2)
    return out + _dot(m01, x3) if terms == 3 else out


def _sel_right(x, m01):
    x1, x2, x3 = _split3(x)
    return _dot(x1, m01) + (_dot(x2, m01) + _dot(x3, m01))


def _sigmoid(x):
    return 1.0 / (1.0 + jnp.exp(-x))


def _silu(x):
    return x * _sigmoid(x)


def _rms(x, g):
    return x * lax.rsqrt(jnp.mean(x * x, axis=-1, keepdims=True) + NORM_EPS) * g


def _vmem_spec():
    return pl.BlockSpec(memory_space=pltpu.VMEM)


def _params(sem):
    return pltpu.CompilerParams(dimension_semantics=sem, vmem_limit_bytes=VMEM_LIMIT_BYTES)


def _inproj_kernel(h_ref, g_ref, w_ref, z_ref):
    y = _rms(h_ref[...], g_ref[...])
    z_ref[...] = _dot(y.astype(BF16), w_ref[...])


def _inproj(h, g, w_bf):
    return pl.pallas_call(
        _inproj_kernel,
        grid=(SEQ // ROW_TILE,),
        in_specs=[pl.BlockSpec((ROW_TILE, D_MODEL), lambda i: (i, 0)),
                  _vmem_spec(), _vmem_spec()],
        out_specs=pl.BlockSpec((ROW_TILE, N_IN), lambda i: (i, 0)),
        out_shape=jax.ShapeDtypeStruct((SEQ, N_IN), F32),
        compiler_params=_params(("arbitrary",)),
        name="inproj",
    )(h, g, w_bf)


def _hgrn_kernel(q_ref, f_ref, i_ref, g_ref, lbp_ref, gn_ref, tri_ref, o_ref,
                 st_s, b_s, qq_s, kk_s, oi_s, *, layer):
    C, T, D = HG_CHUNK, HG_TILE, HG_DIM

    @pl.when(pl.program_id(0) == 0)
    def _():
        st_s[...] = jnp.zeros_like(st_s)

    lbp = lbp_ref[...]
    e = jnp.exp(lbp - jnp.max(lbp, axis=0, keepdims=True))
    sm = e / jnp.sum(e, axis=0, keepdims=True)
    lb = jnp.zeros((1, HG_WIDTH), F32)
    for j in range(1, layer + 1):
        lb = lb + sm[j:j + 1, :]

    q = _silu(q_ref[...])
    f = lb + (1.0 - lb) * _sigmoid(f_ref[...])
    kk = 1.0 - f
    lf = jnp.log(f)
    b = _sel_left(tri_ref[...], lf, terms=2)
    v = i_ref[...]
    b_s[...] = b
    qq_s[...] = q
    kk_s[...] = kk
    safe = jnp.min(b) > -HG_SAFE_EXP
    qe = (q * jnp.exp(b)).astype(BF16)
    kt = (kk * jnp.exp(-jnp.maximum(b, -HG_SAFE_EXP))).astype(BF16)
    vb = v.astype(BF16)
    half = lax.broadcasted_iota(jnp.int32, (2 * C, D), 0) < C
    causal = (lax.broadcasted_iota(jnp.int32, (C, C), 1) <= lax.broadcasted_iota(jnp.int32, (C, C), 0))

    heads = [slice(h * D, (h + 1) * D) for h in range(HG_HEADS)]
    units = [(n, h) for n in range(T // C) for h in range(HG_HEADS)]
    rows = [slice(n * C, (n + 1) * C) for n in range(T // C)]
    vts = [v[:, lanes].T.astype(BF16) for lanes in heads]
    tot = {(n, h): b[(n + 1) * C - 1:(n + 1) * C, heads[h]] for n, h in units}
    scores = {(n, h): _dot(qe[rows[n], heads[h]], kt[rows[n], heads[h]], NT) for n, h in units}
    intra = {(n, h): _dot(jnp.where(causal, scores[n, h], 0.0).astype(BF16), vb[rows[n], heads[h]])
             for n, h in units}
    upd = {}
    for n, h in units:
        slab = slice((n // 2) * 2 * C, (n // 2 + 1) * 2 * C)
        first = half if n % 2 == 0 else jnp.logical_not(half)
        ke = jnp.where(first, kk[slab, heads[h]] * jnp.exp(tot[n, h] - b[slab, heads[h]]), 0.0)
        upd[n, h] = _dot(vts[h][:, slab], ke.astype(BF16))
    state = {(0, h): st_s[h] for h in range(HG_HEADS)}
    for n, h in units:
        state[n + 1, h] = state[n, h] * jnp.exp(tot[n, h]) + upd[n, h]
    for h in range(HG_HEADS):
        st_s[h] = state[T // C, h]
    for n, h in units:
        inter = _dot(qe[rows[n], heads[h]], state[n, h].astype(BF16), NT)
        oi_s[rows[n], heads[h]] = inter
        o_ref[rows[n], heads[h]] = inter + intra[n, h]

    @pl.when(jnp.logical_not(safe))
    def _():
        rid = lax.broadcasted_iota(jnp.int32, (C, D), 0)

        def pick(x, i):
            return jnp.sum(jnp.where(rid == i, x, 0.0), axis=0, keepdims=True)

        for h in range(HG_HEADS):
            lanes = slice(h * D, (h + 1) * D)
            for n in range(T // C):
                rows = slice(n * C, (n + 1) * C)
                b_n, q_n, k_n, v_n = b_s[rows, lanes], qq_s[rows, lanes], kk_s[rows, lanes], i_ref[rows, lanes]

                def row(i, od, b_n=b_n, q_n=q_n, k_n=k_n, v_n=v_n):
                    w = jnp.exp(pick(b_n, i) - b_n) * (pick(q_n, i) * k_n)
                    s = jnp.sum(jnp.where(rid <= i, w, 0.0), axis=-1, keepdims=True)
                    return jnp.where(rid == i, jnp.sum(s * v_n, axis=0, keepdims=True), od)

                o_ref[rows, lanes] = oi_s[rows, lanes] + lax.fori_loop(0, C, row, jnp.zeros((C, D), F32))

    gate = gn_ref[...] * _silu(g_ref[...])
    for h in range(HG_HEADS):
        lanes = slice(h * D, (h + 1) * D)
        o = o_ref[:, lanes]
        o = o * lax.rsqrt(jnp.mean(o * o, axis=-1, keepdims=True) + NORM_EPS)
        o_ref[:, lanes] = o * gate[:, lanes]


def _hgrn(z, lb_params, gnorm_g, layer):
    T, C = HG_TILE, HG_CHUNK
    assert HG_DIM == LANES and T % (2 * C) == 0
    t = np.arange(T)
    tri = jnp.asarray(((t[:, None] // C) == (t[None, :] // C)) & (t[None, :] <= t[:, None]), BF16)
    col = lambda j: pl.BlockSpec((T, HG_WIDTH), lambda i: (i, j))
    scr = pltpu.VMEM((T, HG_WIDTH), F32)
    return pl.pallas_call(
        functools.partial(_hgrn_kernel, layer=layer),
        grid=(SEQ // T,),
        in_specs=[col(0), col(1), col(2), col(3), _vmem_spec(), _vmem_spec(), _vmem_spec()],
        out_specs=pl.BlockSpec((T, HG_WIDTH), lambda i: (i, 0)),
        out_shape=jax.ShapeDtypeStruct((SEQ, HG_WIDTH), F32),
        scratch_shapes=[pltpu.VMEM((HG_HEADS, HG_DIM, HG_DIM), F32), scr, scr, scr, scr],
        compiler_params=_params(("arbitrary",)),
        name="hgrn2",
    )(z, z, z, z, lb_params, gnorm_g, tri)


def _attn_kernel(*refs):
    nh = SW_WIDTH // LANES
    q_ref, ko_ref, vo_ref, kp_ref, vp_ref = (refs[j * nh:(j + 1) * nh] for j in range(5))
    hm_ref, o_ref = refs[5 * nh], refs[5 * nh + 1]
    acc_s, m_s, l_s = (refs[5 * nh + 2 + j * nh:5 * nh + 2 + (j + 1) * nh] for j in range(3))

    def ld(halves, rows):
        return jnp.concatenate([h[rows, :] for h in halves], axis=1)

    def st(halves, rows, val):
        for j, h in enumerate(halves):
            h[rows, :] = val[:, j * LANES:(j + 1) * LANES]

    B = SW_BLOCK
    i = pl.program_id(0)
    hm = hm_ref[...].astype(BF16)
    shape = (SW_HEADS * B, 2 * B)
    qi = lax.broadcasted_iota(jnp.int32, shape, 0) & (B - 1)
    kj = lax.broadcasted_iota(jnp.int32, shape, 1)
    band = (kj >= qi) & (kj <= qi + B)
    own_half = kj >= B
    lane_head = lax.broadcasted_iota(jnp.int32, (B, SW_WIDTH), 1) // SW_HDIM

    for pidx, (window, dil) in enumerate(SW_PATTERNS):
        per_res = AT_SPAN // (dil * B)

        def rows_at(s, dil=dil):
            if dil == 1:
                return pl.ds(pl.multiple_of(s, B), B)
            return pl.ds(s, B, stride=dil)

        def scores(u, dil=dil, per_res=per_res):
            r = u // per_res
            mb = u % per_res
            start = r + dil * B * mb
            rows = rows_at(start)
            q = ld(q_ref, rows) * (SW_HDIM ** -0.5)
            prow = pl.ds(AT_SPAN - dil * B + r, B, stride=dil) if dil > 1 else pl.ds(AT_SPAN - B, B)
            k_prev = ld(kp_ref, prow)
            v_prev = ld(vp_ref, prow)
            if per_res > 1:
                inner = mb > 0
                srows = rows_at(jnp.where(inner, start - dil * B, start))
                k_prev = jnp.where(inner, ld(ko_ref, srows), k_prev)
                v_prev = jnp.where(inner, ld(vo_ref, srows), v_prev)
                valid = band & (own_half | inner | (i > 0))
            else:
                valid = band & (own_half | (i > 0))
            kw = jnp.concatenate([k_prev, ld(ko_ref, rows)], axis=0).astype(BF16)
            vw = jnp.concatenate([v_prev, ld(vo_ref, rows)], axis=0).astype(BF16)
            q4 = jnp.concatenate([q.astype(BF16)] * SW_HEADS, axis=0) * hm
            return rows, jnp.where(valid, _dot(q4, kw, NT), ATTN_MASKED), vw

        def softmax(s):
            m = jnp.max(s, axis=-1, keepdims=True)
            pe = jnp.exp(s - m)
            return m, pe.astype(BF16), jnp.sum(pe, axis=-1, keepdims=True)

        def merge(rows, m, l, pv, pidx=pidx):
            acc = pv[0:B, :]
            mf = jnp.broadcast_to(m[0:B, :], (B, SW_WIDTH))
            lf = jnp.broadcast_to(l[0:B, :], (B, SW_WIDTH))
            for h in range(1, SW_HEADS):
                hs = slice(h * B, (h + 1) * B)
                sel = lane_head == h
                acc = jnp.where(sel, pv[hs, :], acc)
                mf = jnp.where(sel, m[hs, :], mf)
                lf = jnp.where(sel, l[hs, :], lf)
            if pidx > 0:
                m_old = ld(m_s, rows)
                m_new = jnp.maximum(m_old, mf)
                a_old = jnp.exp(m_old - m_new)
                a_cur = jnp.exp(mf - m_new)
                acc = ld(acc_s, rows) * a_old + acc * a_cur
                lf = ld(l_s, rows) * a_old + lf * a_cur
                mf = m_new
            return acc, mf, lf

        def group(gi, carry):
            units = [gi * AT_GROUP + t for t in range(AT_GROUP)]
            sc = [scores(u) for u in units]
            sm = [softmax(s) for _, s, _ in sc]
            pv = [_dot(pe, vw) for (_, pe, _), (_, _, vw) in zip(sm, sc)]
            out = [merge(rows, m, l, p) for (rows, _, _), (m, _, l), p in zip(sc, sm, pv)]
            for (rows, _, _), (acc, mf, lf) in zip(sc, out):
                st(acc_s, rows, acc)
                st(m_s, rows, mf)
                st(l_s, rows, lf)
            return carry

        lax.fori_loop(0, AT_SPAN // (B * AT_GROUP), group, 0)

    for j in range(nh):
        o_ref[:, j * LANES:(j + 1) * LANES] = acc_s[j][...] / l_s[j][...]


def _dilated_mix(z):
    for window, dil in SW_PATTERNS:
        assert window // dil == SW_BLOCK and AT_SPAN % (dil * SW_BLOCK) == 0
    nh = SW_WIDTH // LANES
    qcol = (HG_HEADS * 4 * HG_DIM) // LANES
    own = lambda c: pl.BlockSpec((AT_SPAN, LANES), lambda i: (i, qcol + c))
    prev = lambda c: pl.BlockSpec((AT_SPAN, LANES), lambda i: (jnp.maximum(i - 1, 0), qcol + c))
    row_head = np.arange(SW_HEADS * SW_BLOCK) // SW_BLOCK
    lane_head = np.arange(SW_WIDTH) // SW_HDIM
    hm = jnp.asarray(row_head[:, None] == lane_head[None, :], F32)
    scr = pltpu.VMEM((AT_SPAN, LANES), F32)
    return pl.pallas_call(
        _attn_kernel,
        grid=(SEQ // AT_SPAN,),
        in_specs=[own(c) for c in range(3 * nh)] + [prev(c) for c in range(nh, 3 * nh)] + [_vmem_spec()],
        out_specs=pl.BlockSpec((AT_SPAN, SW_WIDTH), lambda i: (i, 0)),
        out_shape=jax.ShapeDtypeStruct((SEQ, SW_WIDTH), F32),
        scratch_shapes=[scr] * (3 * nh),
        compiler_params=_params(("arbitrary",)),
        name="dilated_attn",
    )(*([z] * (5 * nh)), hm)


def _rwkv_masks():
    t = np.arange(RK_CHUNK)[:, None]
    s = np.tile(np.arange(RK_CHUNK), RK_HEADS)[None, :]
    masks = [s < t, s <= t, s == t]
    for lv in range(RK_LEVELS):
        masks.append(((t >> (lv + 1)) == (s >> (lv + 1))) & (((t >> lv) & 1) == 1) & (((s >> lv) & 1) == 0))
    key_eye = np.arange(RK_HDIM)[:, None] == np.tile(np.arange(RK_HDIM), RK_HEADS)[None, :]
    return np.stack(masks).astype(np.float32), key_eye.astype(np.float32)


def _head_mask(row_block, col_block):
    rh = np.arange(RK_HEADS * row_block) // row_block
    ch = np.arange(RK_HEADS * col_block) // col_block
    return rh[:, None] == ch[None, :]


M_STRICT, M_LOWER, M_EYE, M_LEVEL0 = 0, 1, 2, 3


def _rwkv_kernel(*refs, has_vres):
    (zr_ref, zk_ref, zv_ref, zl_ref, mu_ref, w0_ref, a0_ref, w2_ref, a2_ref, g2_ref,
     kkw_ref, kaw_ref, rkw_ref, lnw_ref, lnb_ref, bd_ref, tri_ref, msk_ref, keye_ref,
     hmn_ref, hmp_ref) = refs[:21]
    refs = refs[21:]
    if has_vres:
        vf_ref, v0_ref, v1_ref, v2_ref = refs[:4]
        refs = refs[4:]
        y_ref, st_s, cb_s = refs
    else:
        y_ref, vfo_ref, st_s, cb_s = refs
    C, T, HD = RK_CHUNK, RK_TILE, RK_HDIM

    @pl.when(pl.program_id(0) == 0)
    def _():
        st_s[...] = jnp.zeros_like(st_s)
        cb_s[...] = jnp.zeros_like(cb_s)

    def shift_mix(idx, ref):
        c = ref[...]
        cb_s[idx, 8:8 + T, :] = c
        cp = cb_s[idx, 7:7 + T, :]
        cb_s[idx, 0:8, :] = c[T - 8:T, :]
        return c + (cp - c) * mu_ref[idx:idx + 1, :]

    r = shift_mix(0, zr_ref)
    k0 = shift_mix(1, zk_ref)
    v = shift_mix(2, zv_ref)
    lo = shift_mix(3, zl_ref)

    bd = bd_ref[...]
    wpre = w0_ref[...] + _mm(jnp.tanh(lo), w2_ref[...])
    sp = jnp.maximum(-wpre, 0.0) + jnp.log(1.0 + jnp.exp(-jnp.abs(wpre)))
    logw = -jnp.exp(-sp - 0.5)
    a = _sigmoid(a0_ref[...] + _mm(lo, a2_ref[...]))
    g = _mm(_sigmoid(lo), g2_ref[...])
    if has_vres:
        mix = _sigmoid(v0_ref[...] + _mm(_mm(v, v1_ref[...]), v2_ref[...]))
        v = v + (vf_ref[...] - v) * mix
    else:
        vfo_ref[...] = v
    kk = k0 * kkw_ref[...]
    kk = kk / jnp.maximum(jnp.sqrt(_sel_right(kk * kk, bd)), 1e-12)
    k = k0 * (1.0 + (a - 1.0) * kaw_ref[...])
    beta = kk * a

    lw = _sel_left(tri_ref[...], logw)
    e_neg = jnp.exp(-lw)
    kap = kk * jnp.exp(lw - logw)
    bh = beta * e_neg
    kh = k * e_neg
    rh = r * jnp.exp(lw)
    hmn = hmn_ref[...]
    hmp = hmp_ref[...]
    hmn_f = hmn.astype(F32)

    def stack_n(x):
        return jnp.concatenate([x.astype(BF16)] * RK_HEADS, axis=0) * hmn

    def stack_p(x):
        return jnp.concatenate([x.astype(BF16)] * RK_HEADS, axis=0) * hmp

    def stack_s(x):
        return jnp.concatenate([x.astype(BF16)] * RK_HEADS, axis=0) * bd

    def packed_t(x):
        full = (jnp.concatenate([x] * RK_HEADS, axis=0) * hmn_f).T
        out = full[0:HD, :]
        for h in range(1, RK_HEADS):
            out = out + full[h * HD:(h + 1) * HD, :]
        return out

    def mm(x, w, dims=NN):
        return _dot(x.astype(BF16), w, dims)

    chunks = range(T // C)
    rows = [slice(c * C, (c + 1) * C) for c in chunks]
    each = lambda fn, *lists: [fn(*args) for args in zip(*lists)]
    lwc = [lw[(c + 1) * C - 1:(c + 1) * C, :] for c in chunks]
    e_end = [jnp.exp(lwc[c] - lw[rows[c], :]) for c in chunks]
    kap_c = [kap[rw, :] for rw in rows]
    rh_c = [rh[rw, :] for rw in rows]
    bh4 = [stack_n(bh[rw, :]) for rw in rows]
    kh4 = [stack_n(kh[rw, :]) for rw in rows]
    v4 = [stack_n(v[rw, :]) for rw in rows]
    rcat = lambda *xs: jnp.concatenate(xs, axis=0)
    kap_rh = each(rcat, kap_c, rh_c)
    with_b = each(lambda x, w: mm(x, w, NT), kap_rh, bh4)
    with_k = each(lambda x, w: mm(x, w, NT), kap_rh, kh4)
    g_kb = [t[0:C, :] for t in with_b]
    a_rb = [t[C:2 * C, :] * msk_ref[M_LOWER] for t in with_b]
    a_k = [t[0:C, :] * msk_ref[M_STRICT] for t in with_k]
    a_rk = [t[C:2 * C, :] * msk_ref[M_LOWER] for t in with_k]
    x = [msk_ref[M_EYE] - gc * msk_ref[M_LEVEL0] for gc in g_kb]
    for lv in range(1, RK_LEVELS):
        xl = each(lambda xc, gc: mm(xc, stack_p(gc * msk_ref[M_LEVEL0 + lv])), x, g_kb)
        x = each(lambda xc, xlc: xc - mm(xlc, stack_p(xc)), x, xl)
    m1 = each(lambda xc, kc: mm(xc, stack_n(kc)), x, kap_c)
    bwt = [packed_t(beta[rows[c], :] * e_end[c]) for c in chunks]
    kwt = [packed_t(k[rows[c], :] * e_end[c]) for c in chunks]
    with_v = each(lambda ak, ark, kw, vc: mm(rcat(ak, ark, kw), vc), a_k, a_rk, kwt, v4)
    m2 = each(lambda xc, t: mm(xc, stack_n(t[0:C, :])), x, with_v)
    arb_bwt = each(rcat, a_rb, bwt)
    with_m1 = each(lambda l, mc: mm(l, stack_n(mc)), arb_bwt, m1)
    with_m2 = each(lambda l, mc: mm(l, stack_n(mc)), arb_bwt, m2)
    q1 = each(lambda rc, t: rc - t[0:C, :], rh_c, with_m1)
    q2 = each(lambda tv, t: tv[C:2 * C, :] - t[0:C, :], with_v, with_m2)
    pt = each(lambda lc, t: keye_ref[...] * jnp.exp(lc) - t[C:C + HD, :], lwc, with_m1)
    rt = each(lambda tv, t: tv[2 * C:2 * C + HD, :] - t[C:C + HD, :], with_v, with_m2)

    st = st_s[...]
    ys = []
    for c in chunks:
        with_st = mm(rcat(q1[c], pt[c]), stack_s(st))
        ys.append(with_st[0:C, :] + q2[c])
        st = with_st[C:C + HD, :] + rt[c]
    st_s[...] = st
    y = jnp.concatenate(ys, axis=0)

    inv_n = 1.0 / RK_HDIM
    mean = _sel_right(y, bd) * inv_n
    d = y - mean
    var = _sel_right(d * d, bd) * inv_n
    yn = d * lax.rsqrt(var + RK_GN_EPS) * lnw_ref[...] + lnb_ref[...]
    bonus = _sel_right(r * k * rkw_ref[...], bd) * v
    y_ref[...] = (yn + bonus) * g


def _rwkv_params(w, l):
    row = lambda a: a.reshape(1, -1)
    n_dec, n_a = w["rwkv_w2"].shape[1], w["rwkv_a2"].shape[1]
    prm = {
        "mu": w["rwkv_mu"][l].reshape(4, RK_WIDTH),
        "w0": row(w["rwkv_w0"][l]), "a0": row(w["rwkv_a0"][l]),
        "w2": _pad_rows(w["rwkv_w2"][l], 0, RK_WIDTH).astype(BF16),
        "a2": _pad_rows(w["rwkv_a2"][l], n_dec, RK_WIDTH).astype(BF16),
        "g2": _pad_rows(w["rwkv_g2"][l], n_dec + n_a, RK_WIDTH).astype(BF16),
        "k_k": row(w["rwkv_k_k"][l]), "k_a": row(w["rwkv_k_a"][l]), "r_k": row(w["rwkv_r_k"][l]),
        "ln_w": row(w["rwkv_ln_w"][l]), "ln_b": row(w["rwkv_ln_b"][l]),
    }
    if l > 0:
        v1, v2 = w["rwkv_v1"][l - 1], w["rwkv_v2"][l - 1]
        prm["v0"] = row(w["rwkv_v0"][l - 1])
        prm["v1"] = jnp.pad(v1, ((0, 0), (0, LANES - v1.shape[1]))).astype(BF16)
        prm["v2"] = _pad_rows(v2, 0, LANES).astype(BF16)
    return prm


def _rwkv(z, prm, v_first):
    has_vres = v_first is not None
    C, T = RK_CHUNK, RK_TILE
    assert (1 << RK_LEVELS) == C and T % C == 0 and RK_HEADS * RK_HDIM == RK_WIDTH
    ccol = (HG_HEADS * 4 * HG_DIM + 3 * SW_WIDTH) // RK_WIDTH
    zcol = lambda off: pl.BlockSpec((T, RK_WIDTH), lambda i: (i, ccol + off))
    row = pl.BlockSpec((T, RK_WIDTH), lambda i: (i, 0))
    lane = np.arange(RK_WIDTH)
    bd = jnp.asarray((lane[:, None] // RK_HDIM) == (lane[None, :] // RK_HDIM), BF16)
    t = np.arange(T)
    tri = jnp.asarray(((t[:, None] // C) == (t[None, :] // C)) & (t[None, :] <= t[:, None]), BF16)
    packed_masks, key_eye = _rwkv_masks()
    args = [z, z, z, z, prm["mu"], prm["w0"], prm["a0"], prm["w2"], prm["a2"], prm["g2"],
            prm["k_k"], prm["k_a"], prm["r_k"], prm["ln_w"], prm["ln_b"], bd, tri,
            jnp.asarray(packed_masks), jnp.asarray(key_eye),
            jnp.asarray(_head_mask(C, RK_HDIM), BF16), jnp.asarray(_head_mask(C, C), BF16)]
    in_specs = [zcol(0), zcol(1), zcol(2), zcol(3)] + [_vmem_spec()] * 17
    if has_vres:
        args += [v_first, prm["v0"], prm["v1"], prm["v2"]]
        in_specs += [row] + [_vmem_spec()] * 3
        out_specs = row
        out_shape = jax.ShapeDtypeStruct((SEQ, RK_WIDTH), F32)
    else:
        out_specs = [row, row]
        out_shape = [jax.ShapeDtypeStruct((SEQ, RK_WIDTH), F32)] * 2
    out = pl.pallas_call(
        functools.partial(_rwkv_kernel, has_vres=has_vres),
        grid=(SEQ // T,),
        in_specs=in_specs,
        out_specs=out_specs,
        out_shape=out_shape,
        scratch_shapes=[pltpu.VMEM((RK_HDIM, RK_WIDTH), F32), pltpu.VMEM((4, T + 8, RK_WIDTH), F32)],
        compiler_params=_params(("arbitrary",)),
        name="rwkv7",
    )(*args)
    if has_vres:
        return out, v_first
    return out[0], out[1]


def _post_kernel(h_ref, oa_ref, ob_ref, oc_ref, p_ref, wout_ref, gffn_ref, wup_ref, cw_ref, cb_ref,
                 wdn_ref, gple_ref, wg_ref, wp_ref, gfin_ref, out_ref,
                 hn_s, carry_s, act_s, *, final):
    T = ROW_TILE

    @pl.when(pl.program_id(0) == 0)
    def _():
        carry_s[...] = jnp.zeros_like(carry_s)

    mix = (_dot(oa_ref[...].astype(BF16), wout_ref[0:HG_WIDTH, :])
           + _dot(ob_ref[...].astype(BF16), wout_ref[HG_WIDTH:HG_WIDTH + SW_WIDTH, :])
           + _dot(oc_ref[...].astype(BF16), wout_ref[HG_WIDTH + SW_WIDTH:, :]))
    h1 = h_ref[...] + mix
    hn_s[...] = _rms(h1, gffn_ref[...]).astype(BF16)

    rid = lax.broadcasted_iota(jnp.int32, (8, FFN_COLS), 0)

    def up(col):
        return _dot(hn_s[...], wup_ref[:, pl.ds(col, FFN_COLS)])

    def conv(col, u):
        cols = pl.ds(col, FFN_COLS)
        prev = carry_s[:, cols]
        carry_s[:, cols] = u[T - 8:T, :]
        r1 = pltpu.roll(u, 1, 0)
        r2 = pltpu.roll(u, 2, 0)
        top1 = jnp.where(rid == 0, prev[7:8, :], r1[0:8, :])
        top2 = jnp.where(rid == 0, prev[6:7, :], jnp.where(rid == 1, prev[7:8, :], r2[0:8, :]))
        u1 = jnp.concatenate([top1, r1[8:, :]], axis=0)
        u2 = jnp.concatenate([top2, r2[8:, :]], axis=0)
        return cb_ref[:, cols] + cw_ref[0:1, cols] * u2 + cw_ref[1:2, cols] * u1 + cw_ref[2:3, cols] * u

    def up_pair(j):
        return up(j * FFN_COLS), up(j * FFN_COLS + D_FF)

    n_steps = D_FF // FFN_COLS
    u_gate, u_val = up_pair(0)
    for j in range(n_steps):
        nxt = up_pair(j + 1) if j + 1 < n_steps else None
        gate = conv(j * FFN_COLS, u_gate)
        val = conv(j * FFN_COLS + D_FF, u_val)
        act_s[:, pl.ds(j * FFN_COLS, FFN_COLS)] = (_silu(gate) * val).astype(BF16)
        if nxt is not None:
            u_gate, u_val = nxt

    h2 = h1 + _dot(act_s[...], wdn_ref[...])
    gate = _sigmoid(_dot(_rms(h2, gple_ref[...]).astype(BF16), wg_ref[...]))
    pe = _dot(p_ref[...].astype(BF16), wp_ref[...])
    h3 = h2 + gate * pe
    out_ref[...] = _rms(h3, gfin_ref[...]) if final else h3


def _post(h, o_a, o_b, o_c, p_l, prm, final):
    T = ROW_TILE
    tile = lambda w: pl.BlockSpec((T, w), lambda i: (i, 0))
    return pl.pallas_call(
        functools.partial(_post_kernel, final=final),
        grid=(SEQ // T,),
        in_specs=[tile(D_MODEL), tile(HG_WIDTH), tile(SW_WIDTH), tile(RK_WIDTH), tile(PLE_DIM)]
                 + [_vmem_spec()] * 10,
        out_specs=tile(D_MODEL),
        out_shape=jax.ShapeDtypeStruct((SEQ, D_MODEL), F32),
        scratch_shapes=[pltpu.VMEM((T, D_MODEL), BF16),
                        pltpu.VMEM((8, 2 * D_FF), F32), pltpu.VMEM((T, D_FF), BF16)],
        compiler_params=_params(("arbitrary",)),
        name="outproj_ffn_ple",
    )(h, o_a, o_b, o_c, p_l, prm["w_out"], prm["g_ffn"], prm["w_up"], prm["conv_w"], prm["conv_b"],
      prm["w_down"], prm["g_ple"], prm["w_gate"], prm["w_ple"], prm["g_final"])


def _pad_rows(w, start, total):
    out = jnp.zeros((total, w.shape[1]), w.dtype)
    return lax.dynamic_update_slice(out, w, (start, 0))


def kernel(x, p, w_in, w_out, norm_mix_g, norm_ffn_g, norm_ple_g, final_norm_g, hgrn_lower_bounds, hgrn_gnorm_g, rwkv_mu, rwkv_w0, rwkv_w2, rwkv_a0, rwkv_a2, rwkv_g2, rwkv_k_k, rwkv_k_a, rwkv_r_k, rwkv_ln_w, rwkv_ln_b, rwkv_v0, rwkv_v1, rwkv_v2, ffn_up, ffn_conv_w, ffn_conv_b, ffn_down, ple_proj, ple_gate):
    row = lambda a: a.reshape(1, -1)
    h = x.reshape(SEQ, D_MODEL)
    v_first = None
    rwkv_w = dict(rwkv_mu=rwkv_mu, rwkv_w0=rwkv_w0, rwkv_w2=rwkv_w2, rwkv_a0=rwkv_a0, rwkv_a2=rwkv_a2,
                  rwkv_g2=rwkv_g2, rwkv_k_k=rwkv_k_k, rwkv_k_a=rwkv_k_a, rwkv_r_k=rwkv_r_k,
                  rwkv_ln_w=rwkv_ln_w, rwkv_ln_b=rwkv_ln_b, rwkv_v0=rwkv_v0, rwkv_v1=rwkv_v1, rwkv_v2=rwkv_v2)
    for l in range(DEPTH):
        z = _inproj(h, row(norm_mix_g[l]), w_in[l].astype(BF16))
        o_a = _hgrn(z, hgrn_lower_bounds, row(hgrn_gnorm_g[l]), l)
        o_b = _dilated_mix(z)
        o_c, v_first = _rwkv(z, _rwkv_params(rwkv_w, l), v_first)
        post = {
            "w_out": w_out[l].astype(BF16), "g_ffn": row(norm_ffn_g[l]),
            "w_up": ffn_up[l].astype(BF16), "conv_w": ffn_conv_w[l], "conv_b": row(ffn_conv_b[l]),
            "w_down": ffn_down[l].astype(BF16), "g_ple": row(norm_ple_g[l]),
            "w_gate": ple_gate[l].astype(BF16), "w_ple": ple_proj[l].astype(BF16),
            "g_final": row(final_norm_g),
        }
        h = _post(h, o_a, o_b, o_c, p[l].reshape(SEQ, PLE_DIM), post, final=l == DEPTH - 1)
    return h.reshape(1, SEQ, D_MODEL)
```

```python
import functools

import numpy as np
import jax
import jax.numpy as jnp
from jax import lax
from jax.experimental import pallas as pl
from jax.experimental.pallas import tpu as pltpu

F32 = jnp.float32
BF16 = jnp.bfloat16

D_MODEL = 1024
SEQ = 16384
DEPTH = 2
HG_HEADS = 4
HG_DIM = 128
HG_WIDTH = 512
SW_HEADS = 4
SW_HDIM = 64
SW_WIDTH = 256
SW_PATTERNS = ((128, 1), (512, 4), (2048, 16))
SW_BLOCK = 128
RK_HEADS = 4
RK_HDIM = 64
RK_WIDTH = 256
RK_GN_EPS = 64e-5
N_IN = 3840
D_FF = 2816
PLE_DIM = 256
NORM_EPS = 1e-6

LANES = 128
MXU_DIM = 256
VMEM_LIMIT_BYTES = 56 * 1024 * 1024

ROW_TILE = 512
FFN_COLS = 256
HG_TILE = 256
HG_CHUNK = 64
HG_SAFE_EXP = 80.0
AT_SPAN = SW_BLOCK * max(d for _, d in SW_PATTERNS)
ATTN_MASKED = -1e30
AT_GROUP = 2
RK_CHUNK = 64
RK_TILE = 512
RK_LEVELS = 6
RK_CUM = MXU_DIM

NN = (((1,), (0,)), ((), ()))
NT = (((1,), (1,)), ((), ()))


def _dot(a, b, dims=NN):
    return lax.dot_general(a, b, dims, preferred_element_type=F32)


def _mm(a, b, dims=NN):
    return _dot(a.astype(BF16), b.astype(BF16), dims)


def _split3(x):
    x1 = x.astype(BF16)
    r1 = x - x1.astype(F32)
    x2 = r1.astype(BF16)
    x3 = (r1 - x2.astype(F32)).astype(BF16)
    return x1, x2, x3


def _sel_left(m01, x, terms=3):
    x1, x2, x3 = _split3(x)
    out = _dot(m01, x1) + _dot(m01, x2)
    return out + _dot(m01, x3) if terms == 3 else out


def _sel_right(x, m01, terms=3):
    x1, x2, x3 = _split3(x)
    out = _dot(x1, m01) + _dot(x2, m01)
    return out + _dot(x3, m01) if terms == 3 else out


def _sigmoid(x):
    return 1.0 / (1.0 + jnp.exp(-x))


def _silu(x):
    return x * _sigmoid(x)


def _rms(x, g):
    return x * lax.rsqrt(jnp.mean(x * x, axis=-1, keepdims=True) + NORM_EPS) * g


def _vmem_spec():
    return pl.BlockSpec(memory_space=pltpu.VMEM)


def _params(sem):
    return pltpu.CompilerParams(dimension_semantics=sem, vmem_limit_bytes=VMEM_LIMIT_BYTES)


def _inproj_kernel(h_ref, g_ref, w_ref, z_ref):
    y = _rms(h_ref[...], g_ref[...])
    z_ref[...] = _dot(y.astype(BF16), w_ref[...])


def _inproj(h, g, w_bf):
    return pl.pallas_call(
        _inproj_kernel,
        grid=(SEQ // ROW_TILE,),
        in_specs=[pl.BlockSpec((ROW_TILE, D_MODEL), lambda i: (i, 0)),
                  _vmem_spec(), _vmem_spec()],
        out_specs=pl.BlockSpec((ROW_TILE, N_IN), lambda i: (i, 0)),
        out_shape=jax.ShapeDtypeStruct((SEQ, N_IN), F32),
        compiler_params=_params(("arbitrary",)),
        name="inproj",
    )(h, g, w_bf)


def _hgrn_kernel(q_ref, f_ref, i_ref, g_ref, lbp_ref, gn_ref, tri_ref, o_ref,
                 st_s, b_s, qq_s, kk_s, oi_s, *, layer):
    C, T, D = HG_CHUNK, HG_TILE, HG_DIM

    @pl.when(pl.program_id(0) == 0)
    def _():
        st_s[...] = jnp.zeros_like(st_s)

    lbp = lbp_ref[...]
    e = jnp.exp(lbp - jnp.max(lbp, axis=0, keepdims=True))
    sm = e / jnp.sum(e, axis=0, keepdims=True)
    lb = jnp.zeros((1, HG_WIDTH), F32)
    for j in range(1, layer + 1):
        lb = lb + sm[j:j + 1, :]

    q = _silu(q_ref[...])
    f = lb + (1.0 - lb) * _sigmoid(f_ref[...])
    kk = 1.0 - f
    lf = jnp.log(f)
    b = _sel_left(tri_ref[...], lf, terms=2)
    v = i_ref[...]
    b_s[...] = b
    qq_s[...] = q
    kk_s[...] = kk
    safe = jnp.min(b) > -HG_SAFE_EXP
    qe = (q * jnp.exp(b)).astype(BF16)
    kt = (kk * jnp.exp(-jnp.maximum(b, -HG_SAFE_EXP))).astype(BF16)
    vb = v.astype(BF16)
    half = lax.broadcasted_iota(jnp.int32, (2 * C, D), 0) < C
    causal = (lax.broadcasted_iota(jnp.int32, (C, C), 1) <= lax.broadcasted_iota(jnp.int32, (C, C), 0))

    heads = [slice(h * D, (h + 1) * D) for h in range(HG_HEADS)]
    units = [(n, h) for n in range(T // C) for h in range(HG_HEADS)]
    rows = [slice(n * C, (n + 1) * C) for n in range(T // C)]
    vts = [v[:, lanes].T.astype(BF16) for lanes in heads]
    tot = {(n, h): b[(n + 1) * C - 1:(n + 1) * C, heads[h]] for n, h in units}
    scores = {(n, h): _dot(qe[rows[n], heads[h]], kt[rows[n], heads[h]], NT) for n, h in units}
    intra = {(n, h): _dot(jnp.where(causal, scores[n, h], 0.0).astype(BF16), vb[rows[n], heads[h]])
             for n, h in units}
    upd = {}
    for n, h in units:
        slab = slice((n // 2) * 2 * C, (n // 2 + 1) * 2 * C)
        first = half if n % 2 == 0 else jnp.logical_not(half)
        ke = jnp.where(first, kk[slab, heads[h]] * jnp.exp(tot[n, h] - b[slab, heads[h]]), 0.0)
        upd[n, h] = _dot(vts[h][:, slab], ke.astype(BF16))
    state = {(0, h): st_s[h] for h in range(HG_HEADS)}
    for n, h in units:
        state[n + 1, h] = state[n, h] * jnp.exp(tot[n, h]) + upd[n, h]
    for h in range(HG_HEADS):
        st_s[h] = state[T // C, h]
    for n, h in units:
        inter = _dot(qe[rows[n], heads[h]], state[n, h].astype(BF16), NT)
        oi_s[rows[n], heads[h]] = inter
        o_ref[rows[n], heads[h]] = inter + intra[n, h]

    @pl.when(jnp.logical_not(safe))
    def _():
        rid = lax.broadcasted_iota(jnp.int32, (C, D), 0)

        def pick(x, i):
            return jnp.sum(jnp.where(rid == i, x, 0.0), axis=0, keepdims=True)

        for h in range(HG_HEADS):
            lanes = slice(h * D, (h + 1) * D)
            for n in range(T // C):
                rows = slice(n * C, (n + 1) * C)
                b_n, q_n, k_n, v_n = b_s[rows, lanes], qq_s[rows, lanes], kk_s[rows, lanes], i_ref[rows, lanes]

                def row(i, od, b_n=b_n, q_n=q_n, k_n=k_n, v_n=v_n):
                    w = jnp.exp(pick(b_n, i) - b_n) * (pick(q_n, i) * k_n)
                    s = jnp.sum(jnp.where(rid <= i, w, 0.0), axis=-1, keepdims=True)
                    return jnp.where(rid == i, jnp.sum(s * v_n, axis=0, keepdims=True), od)

                o_ref[rows, lanes] = oi_s[rows, lanes] + lax.fori_loop(0, C, row, jnp.zeros((C, D), F32))

    gate = gn_ref[...] * _silu(g_ref[...])
    for h in range(HG_HEADS):
        lanes = slice(h * D, (h + 1) * D)
        o = o_ref[:, lanes]
        o = o * lax.rsqrt(jnp.mean(o * o, axis=-1, keepdims=True) + NORM_EPS)
        o_ref[:, lanes] = o * gate[:, lanes]


def _hgrn(z, lb_params, gnorm_g, layer):
    T, C = HG_TILE, HG_CHUNK
    assert HG_DIM == LANES and T % (2 * C) == 0
    t = np.arange(T)
    tri = jnp.asarray(((t[:, None] // C) == (t[None, :] // C)) & (t[None, :] <= t[:, None]), BF16)
    col = lambda j: pl.BlockSpec((T, HG_WIDTH), lambda i: (i, j))
    scr = pltpu.VMEM((T, HG_WIDTH), F32)
    return pl.pallas_call(
        functools.partial(_hgrn_kernel, layer=layer),
        grid=(SEQ // T,),
        in_specs=[col(0), col(1), col(2), col(3), _vmem_spec(), _vmem_spec(), _vmem_spec()],
        out_specs=pl.BlockSpec((T, HG_WIDTH), lambda i: (i, 0)),
        out_shape=jax.ShapeDtypeStruct((SEQ, HG_WIDTH), F32),
        scratch_shapes=[pltpu.VMEM((HG_HEADS, HG_DIM, HG_DIM), F32), scr, scr, scr, scr],
        compiler_params=_params(("arbitrary",)),
        name="hgrn2",
    )(z, z, z, z, lb_params, gnorm_g, tri)


def _attn_kernel(*refs):
    nh = SW_WIDTH // LANES
    q_ref, ko_ref, vo_ref, kp_ref, vp_ref = (refs[j * nh:(j + 1) * nh] for j in range(5))
    hm_ref, o_ref = refs[5 * nh], refs[5 * nh + 1]
    acc_s, m_s, l_s = (refs[5 * nh + 2 + j * nh:5 * nh + 2 + (j + 1) * nh] for j in range(3))

    def ld(halves, rows):
        return jnp.concatenate([h[rows, :] for h in halves], axis=1)

    def st(halves, rows, val):
        for j, h in enumerate(halves):
            h[rows, :] = val[:, j * LANES:(j + 1) * LANES]

    B = SW_BLOCK
    i = pl.program_id(0)
    hm = hm_ref[...].astype(BF16)
    shape = (SW_HEADS * B, 2 * B)
    qi = lax.broadcasted_iota(jnp.int32, shape, 0) & (B - 1)
    kj = lax.broadcasted_iota(jnp.int32, shape, 1)
    band = (kj >= qi) & (kj <= qi + B)
    own_half = kj >= B
    lane_head = lax.broadcasted_iota(jnp.int32, (B, SW_WIDTH), 1) // SW_HDIM

    for pidx, (window, dil) in enumerate(SW_PATTERNS):
        per_res = AT_SPAN // (dil * B)

        def rows_at(s, dil=dil):
            if dil == 1:
                return pl.ds(pl.multiple_of(s, B), B)
            return pl.ds(s, B, stride=dil)

        def scores(u, dil=dil, per_res=per_res):
            r = u // per_res
            mb = u % per_res
            start = r + dil * B * mb
            rows = rows_at(start)
            q = ld(q_ref, rows) * (SW_HDIM ** -0.5)
            prow = pl.ds(AT_SPAN - dil * B + r, B, stride=dil) if dil > 1 else pl.ds(AT_SPAN - B, B)
            k_prev = ld(kp_ref, prow)
            v_prev = ld(vp_ref, prow)
            if per_res > 1:
                inner = mb > 0
                srows = rows_at(jnp.where(inner, start - dil * B, start))
                k_prev = jnp.where(inner, ld(ko_ref, srows), k_prev)
                v_prev = jnp.where(inner, ld(vo_ref, srows), v_prev)
                valid = band & (own_half | inner | (i > 0))
            else:
                valid = band & (own_half | (i > 0))
            kw = jnp.concatenate([k_prev, ld(ko_ref, rows)], axis=0).astype(BF16)
            vw = jnp.concatenate([v_prev, ld(vo_ref, rows)], axis=0).astype(BF16)
            q4 = jnp.concatenate([q.astype(BF16)] * SW_HEADS, axis=0) * hm
            return rows, jnp.where(valid, _dot(q4, kw, NT), ATTN_MASKED), vw

        def softmax(s):
            m = jnp.max(s, axis=-1, keepdims=True)
            pe = jnp.exp(s - m)
            return m, pe.astype(BF16), jnp.sum(pe, axis=-1, keepdims=True)

        def merge(rows, m, l, pv, pidx=pidx):
            acc = pv[0:B, :]
            mf = jnp.broadcast_to(m[0:B, :], (B, SW_WIDTH))
            lf = jnp.broadcast_to(l[0:B, :], (B, SW_WIDTH))
            for h in range(1, SW_HEADS):
                hs = slice(h * B, (h + 1) * B)
                sel = lane_head == h
                acc = jnp.where(sel, pv[hs, :], acc)
                mf = jnp.where(sel, m[hs, :], mf)
                lf = jnp.where(sel, l[hs, :], lf)
            if pidx > 0:
                m_old = ld(m_s, rows)
                m_new = jnp.maximum(m_old, mf)
                a_old = jnp.exp(m_old - m_new)
                a_cur = jnp.exp(mf - m_new)
                acc = ld(acc_s, rows) * a_old + acc * a_cur
                lf = ld(l_s, rows) * a_old + lf * a_cur
                mf = m_new
            return acc, mf, lf

        def group(gi, carry):
            units = [gi * AT_GROUP + t for t in range(AT_GROUP)]
            sc = [scores(u) for u in units]
            sm = [softmax(s) for _, s, _ in sc]
            pv = [_dot(pe, vw) for (_, pe, _), (_, _, vw) in zip(sm, sc)]
            out = [merge(rows, m, l, p) for (rows, _, _), (m, _, l), p in zip(sc, sm, pv)]
            for (rows, _, _), (acc, mf, lf) in zip(sc, out):
                st(acc_s, rows, acc)
                st(m_s, rows, mf)
                st(l_s, rows, lf)
            return carry

        lax.fori_loop(0, AT_SPAN // (B * AT_GROUP), group, 0)

    for j in range(nh):
        o_ref[:, j * LANES:(j + 1) * LANES] = acc_s[j][...] / l_s[j][...]


def _dilated_mix(z):
    for window, dil in SW_PATTERNS:
        assert window // dil == SW_BLOCK and AT_SPAN % (dil * SW_BLOCK) == 0
    nh = SW_WIDTH // LANES
    qcol = (HG_HEADS * 4 * HG_DIM) // LANES
    own = lambda c: pl.BlockSpec((AT_SPAN, LANES), lambda i: (i, qcol + c))
    prev = lambda c: pl.BlockSpec((AT_SPAN, LANES), lambda i: (jnp.maximum(i - 1, 0), qcol + c))
    row_head = np.arange(SW_HEADS * SW_BLOCK) // SW_BLOCK
    lane_head = np.arange(SW_WIDTH) // SW_HDIM
    hm = jnp.asarray(row_head[:, None] == lane_head[None, :], F32)
    scr = pltpu.VMEM((AT_SPAN, LANES), F32)
    return pl.pallas_call(
        _attn_kernel,
        grid=(SEQ // AT_SPAN,),
        in_specs=[own(c) for c in range(3 * nh)] + [prev(c) for c in range(nh, 3 * nh)] + [_vmem_spec()],
        out_specs=pl.BlockSpec((AT_SPAN, SW_WIDTH), lambda i: (i, 0)),
        out_shape=jax.ShapeDtypeStruct((SEQ, SW_WIDTH), F32),
        scratch_shapes=[scr] * (3 * nh),
        compiler_params=_params(("arbitrary",)),
        name="dilated_attn",
    )(*([z] * (5 * nh)), hm)


def _rwkv_masks():
    t = np.arange(RK_CHUNK)[:, None]
    s = np.tile(np.arange(RK_CHUNK), RK_HEADS)[None, :]
    masks = [s < t, s <= t, s == t]
    for lv in range(RK_LEVELS):
        masks.append(((t >> (lv + 1)) == (s >> (lv + 1))) & (((t >> lv) & 1) == 1) & (((s >> lv) & 1) == 0))
    key_eye = np.arange(RK_HDIM)[:, None] == np.tile(np.arange(RK_HDIM), RK_HEADS)[None, :]
    return np.stack(masks).astype(np.float32), key_eye.astype(np.float32)


def _head_mask(row_block, col_block):
    rh = np.arange(RK_HEADS * row_block) // row_block
    ch = np.arange(RK_HEADS * col_block) // col_block
    return rh[:, None] == ch[None, :]


M_STRICT, M_LOWER, M_EYE, M_LEVEL0 = 0, 1, 2, 3


def _rwkv_kernel(*refs, has_vres):
    (zr_ref, zk_ref, zv_ref, zl_ref, mu_ref, w0_ref, a0_ref, w2_ref, a2_ref, g2_ref,
     kkw_ref, kaw_ref, rkw_ref, lnw_ref, lnb_ref, bd_ref, tri_ref, msk_ref, keye_ref,
     hmn_ref, hmp_ref) = refs[:21]
    refs = refs[21:]
    if has_vres:
        vf_ref, v0_ref, v1_ref, v2_ref = refs[:4]
        refs = refs[4:]
        y_ref, st_s, cb_s = refs
    else:
        y_ref, vfo_ref, st_s, cb_s = refs
    C, T, HD = RK_CHUNK, RK_TILE, RK_HDIM

    @pl.when(pl.program_id(0) == 0)
    def _():
        st_s[...] = jnp.zeros_like(st_s)
        cb_s[...] = jnp.zeros_like(cb_s)

    def shift_mix(idx, ref):
        c = ref[...]
        cb_s[idx, 8:8 + T, :] = c
        cp = cb_s[idx, 7:7 + T, :]
        cb_s[idx, 0:8, :] = c[T - 8:T, :]
        return c + (cp - c) * mu_ref[idx:idx + 1, :]

    r = shift_mix(0, zr_ref)
    k0 = shift_mix(1, zk_ref)
    v = shift_mix(2, zv_ref)
    lo = shift_mix(3, zl_ref)

    bd = bd_ref[...]
    wpre = w0_ref[...] + _mm(jnp.tanh(lo), w2_ref[...])
    sp = jnp.maximum(-wpre, 0.0) + jnp.log(1.0 + jnp.exp(-jnp.abs(wpre)))
    logw = -jnp.exp(-sp - 0.5)
    a = _sigmoid(a0_ref[...] + _mm(lo, a2_ref[...]))
    g = _mm(_sigmoid(lo), g2_ref[...])
    if has_vres:
        mix = _sigmoid(v0_ref[...] + _mm(_mm(v, v1_ref[...]), v2_ref[...]))
        v = v + (vf_ref[...] - v) * mix
    else:
        vfo_ref[...] = v
    kk = k0 * kkw_ref[...]
    kk = kk / jnp.maximum(jnp.sqrt(_sel_right(kk * kk, bd, terms=2)), 1e-12)
    k = k0 * (1.0 + (a - 1.0) * kaw_ref[...])
    beta = kk * a

    lw = jnp.concatenate([_sel_left(tri_ref[...], logw[i:i + RK_CUM, :], terms=2)
                          for i in range(0, T, RK_CUM)], axis=0)
    e_neg = jnp.exp(-lw)
    kap = kk * jnp.exp(lw - logw)
    bh = beta * e_neg
    kh = k * e_neg
    rh = r * jnp.exp(lw)
    hmn = hmn_ref[...]
    hmp = hmp_ref[...]
    hmn_f = hmn.astype(F32)

    def stack_n(x):
        return jnp.concatenate([x.astype(BF16)] * RK_HEADS, axis=0) * hmn

    def stack_p(x):
        return jnp.concatenate([x.astype(BF16)] * RK_HEADS, axis=0) * hmp

    def stack_s(x):
        return jnp.concatenate([x.astype(BF16)] * RK_HEADS, axis=0) * bd

    def packed_t(x):
        full = (jnp.concatenate([x] * RK_HEADS, axis=0) * hmn_f).T
        out = full[0:HD, :]
        for h in range(1, RK_HEADS):
            out = out + full[h * HD:(h + 1) * HD, :]
        return out

    def mm(x, w, dims=NN):
        return _dot(x.astype(BF16), w, dims)

    chunks = range(T // C)
    rows = [slice(c * C, (c + 1) * C) for c in chunks]
    each = lambda fn, *lists: [fn(*args) for args in zip(*lists)]
    lwc = [lw[(c + 1) * C - 1:(c + 1) * C, :] for c in chunks]
    e_end = [jnp.exp(lwc[c] - lw[rows[c], :]) for c in chunks]
    kap_c = [kap[rw, :] for rw in rows]
    rh_c = [rh[rw, :] for rw in rows]
    bh4 = [stack_n(bh[rw, :]) for rw in rows]
    kh4 = [stack_n(kh[rw, :]) for rw in rows]
    v4 = [stack_n(v[rw, :]) for rw in rows]
    rcat = lambda *xs: jnp.concatenate(xs, axis=0)
    kap_rh = each(rcat, kap_c, rh_c)
    with_b = each(lambda x, w: mm(x, w, NT), kap_rh, bh4)
    with_k = each(lambda x, w: mm(x, w, NT), kap_rh, kh4)
    g_kb = [t[0:C, :] for t in with_b]
    a_rb = [t[C:2 * C, :] * msk_ref[M_LOWER] for t in with_b]
    a_k = [t[0:C, :] * msk_ref[M_STRICT] for t in with_k]
    a_rk = [t[C:2 * C, :] * msk_ref[M_LOWER] for t in with_k]
    x = [msk_ref[M_EYE] - gc * msk_ref[M_LEVEL0] for gc in g_kb]
    for lv in range(1, RK_LEVELS):
        xl = each(lambda xc, gc: mm(xc, stack_p(gc * msk_ref[M_LEVEL0 + lv])), x, g_kb)
        x = each(lambda xc, xlc: xc - mm(xlc, stack_p(xc)), x, xl)
    m1 = each(lambda xc, kc: mm(xc, stack_n(kc)), x, kap_c)
    bwt = [packed_t(beta[rows[c], :] * e_end[c]) for c in chunks]
    kwt = [packed_t(k[rows[c], :] * e_end[c]) for c in chunks]
    with_v = each(lambda ak, ark, kw, vc: mm(rcat(ak, ark, kw), vc), a_k, a_rk, kwt, v4)
    m2 = each(lambda xc, t: mm(xc, stack_n(t[0:C, :])), x, with_v)
    arb_bwt = each(rcat, a_rb, bwt)
    with_m1 = each(lambda l, mc: mm(l, stack_n(mc)), arb_bwt, m1)
    with_m2 = each(lambda l, mc: mm(l, stack_n(mc)), arb_bwt, m2)
    q1 = each(lambda rc, t: rc - t[0:C, :], rh_c, with_m1)
    q2 = each(lambda tv, t: tv[C:2 * C, :] - t[0:C, :], with_v, with_m2)
    pt = each(lambda lc, t: keye_ref[...] * jnp.exp(lc) - t[C:C + HD, :], lwc, with_m1)
    rt = each(lambda tv, t: tv[2 * C:2 * C + HD, :] - t[C:C + HD, :], with_v, with_m2)

    st = st_s[...]
    ys = []
    for c in chunks:
        with_st = mm(rcat(q1[c], pt[c]), stack_s(st))
        ys.append(with_st[0:C, :] + q2[c])
        st = with_st[C:C + HD, :] + rt[c]
    st_s[...] = st
    y = jnp.concatenate(ys, axis=0)

    inv_n = 1.0 / RK_HDIM
    mean = _sel_right(y, bd, terms=2) * inv_n
    d = y - mean
    var = _sel_right(d * d, bd, terms=2) * inv_n
    yn = d * lax.rsqrt(var + RK_GN_EPS) * lnw_ref[...] + lnb_ref[...]
    bonus = _sel_right(r * k * rkw_ref[...], bd, terms=2) * v
    y_ref[...] = (yn + bonus) * g


def _rwkv_params(w, l):
    row = lambda a: a.reshape(1, -1)
    n_dec, n_a = w["rwkv_w2"].shape[1], w["rwkv_a2"].shape[1]
    prm = {
        "mu": w["rwkv_mu"][l].reshape(4, RK_WIDTH),
        "w0": row(w["rwkv_w0"][l]), "a0": row(w["rwkv_a0"][l]),
        "w2": _pad_rows(w["rwkv_w2"][l], 0, RK_WIDTH).astype(BF16),
        "a2": _pad_rows(w["rwkv_a2"][l], n_dec, RK_WIDTH).astype(BF16),
        "g2": _pad_rows(w["rwkv_g2"][l], n_dec + n_a, RK_WIDTH).astype(BF16),
        "k_k": row(w["rwkv_k_k"][l]), "k_a": row(w["rwkv_k_a"][l]), "r_k": row(w["rwkv_r_k"][l]),
        "ln_w": row(w["rwkv_ln_w"][l]), "ln_b": row(w["rwkv_ln_b"][l]),
    }
    if l > 0:
        v1, v2 = w["rwkv_v1"][l - 1], w["rwkv_v2"][l - 1]
        prm["v0"] = row(w["rwkv_v0"][l - 1])
        prm["v1"] = jnp.pad(v1, ((0, 0), (0, LANES - v1.shape[1]))).astype(BF16)
        prm["v2"] = _pad_rows(v2, 0, LANES).astype(BF16)
    return prm


def _rwkv(z, prm, v_first):
    has_vres = v_first is not None
    C, T = RK_CHUNK, RK_TILE
    assert (1 << RK_LEVELS) == C and T % C == 0 and RK_HEADS * RK_HDIM == RK_WIDTH
    ccol = (HG_HEADS * 4 * HG_DIM + 3 * SW_WIDTH) // RK_WIDTH
    zcol = lambda off: pl.BlockSpec((T, RK_WIDTH), lambda i: (i, ccol + off))
    row = pl.BlockSpec((T, RK_WIDTH), lambda i: (i, 0))
    lane = np.arange(RK_WIDTH)
    bd = jnp.asarray((lane[:, None] // RK_HDIM) == (lane[None, :] // RK_HDIM), BF16)
    t = np.arange(RK_CUM)
    tri = jnp.asarray(((t[:, None] // C) == (t[None, :] // C)) & (t[None, :] <= t[:, None]), BF16)
    packed_masks, key_eye = _rwkv_masks()
    args = [z, z, z, z, prm["mu"], prm["w0"], prm["a0"], prm["w2"], prm["a2"], prm["g2"],
            prm["k_k"], prm["k_a"], prm["r_k"], prm["ln_w"], prm["ln_b"], bd, tri,
            jnp.asarray(packed_masks), jnp.asarray(key_eye),
            jnp.asarray(_head_mask(C, RK_HDIM), BF16), jnp.asarray(_head_mask(C, C), BF16)]
    in_specs = [zcol(0), zcol(1), zcol(2), zcol(3)] + [_vmem_spec()] * 17
    if has_vres:
        args += [v_first, prm["v0"], prm["v1"], prm["v2"]]
        in_specs += [row] + [_vmem_spec()] * 3
        out_specs = row
        out_shape = jax.ShapeDtypeStruct((SEQ, RK_WIDTH), F32)
    else:
        out_specs = [row, row]
        out_shape = [jax.ShapeDtypeStruct((SEQ, RK_WIDTH), F32)] * 2
    out = pl.pallas_call(
        functools.partial(_rwkv_kernel, has_vres=has_vres),
        grid=(SEQ // T,),
        in_specs=in_specs,
        out_specs=out_specs,
        out_shape=out_shape,
        scratch_shapes=[pltpu.VMEM((RK_HDIM, RK_WIDTH), F32), pltpu.VMEM((4, T + 8, RK_WIDTH), F32)],
        compiler_params=_params(("arbitrary",)),
        name="rwkv7",
    )(*args)
    if has_vres:
        return out, v_first
    return out[0], out[1]


def _post_kernel(h_ref, oa_ref, ob_ref, oc_ref, p_ref, wout_ref, gffn_ref, wup_ref, cw_ref, cb_ref,
                 wdn_ref, gple_ref, wg_ref, wp_ref, gfin_ref, out_ref,
                 hn_s, carry_s, act_s, *, final):
    T = ROW_TILE

    @pl.when(pl.program_id(0) == 0)
    def _():
        carry_s[...] = jnp.zeros_like(carry_s)

    mix = (_dot(oa_ref[...].astype(BF16), wout_ref[0:HG_WIDTH, :])
           + _dot(ob_ref[...].astype(BF16), wout_ref[HG_WIDTH:HG_WIDTH + SW_WIDTH, :])
           + _dot(oc_ref[...].astype(BF16), wout_ref[HG_WIDTH + SW_WIDTH:, :]))
    h1 = h_ref[...] + mix
    hn_s[...] = _rms(h1, gffn_ref[...]).astype(BF16)

    rid = lax.broadcasted_iota(jnp.int32, (8, FFN_COLS), 0)

    def up(col):
        return _dot(hn_s[...], wup_ref[:, pl.ds(col, FFN_COLS)])

    def conv(col, u):
        cols = pl.ds(col, FFN_COLS)
        prev = carry_s[:, cols]
        carry_s[:, cols] = u[T - 8:T, :]
        r1 = pltpu.roll(u, 1, 0)
        r2 = pltpu.roll(u, 2, 0)
        top1 = jnp.where(rid == 0, prev[7:8, :], r1[0:8, :])
        top2 = jnp.where(rid == 0, prev[6:7, :], jnp.where(rid == 1, prev[7:8, :], r2[0:8, :]))
        u1 = jnp.concatenate([top1, r1[8:, :]], axis=0)
        u2 = jnp.concatenate([top2, r2[8:, :]], axis=0)
        return cb_ref[:, cols] + cw_ref[0:1, cols] * u2 + cw_ref[1:2, cols] * u1 + cw_ref[2:3, cols] * u

    def up_pair(j):
        return up(j * FFN_COLS), up(j * FFN_COLS + D_FF)

    n_steps = D_FF // FFN_COLS
    u_gate, u_val = up_pair(0)
    for j in range(n_steps):
        nxt = up_pair(j + 1) if j + 1 < n_steps else None
        gate = conv(j * FFN_COLS, u_gate)
        val = conv(j * FFN_COLS + D_FF, u_val)
        act_s[:, pl.ds(j * FFN_COLS, FFN_COLS)] = (_silu(gate) * val).astype(BF16)
        if nxt is not None:
            u_gate, u_val = nxt

    h2 = h1 + _dot(act_s[...], wdn_ref[...])
    gate = _sigmoid(_dot(_rms(h2, gple_ref[...]).astype(BF16), wg_ref[...]))
    pe = _dot(p_ref[...].astype(BF16), wp_ref[...])
    h3 = h2 + gate * pe
    out_ref[...] = _rms(h3, gfin_ref[...]) if final else h3


def _post(h, o_a, o_b, o_c, p_l, prm, final):
    T = ROW_TILE
    tile = lambda w: pl.BlockSpec((T, w), lambda i: (i, 0))
    return pl.pallas_call(
        functools.partial(_post_kernel, final=final),
        grid=(SEQ // T,),
        in_specs=[tile(D_MODEL), tile(HG_WIDTH), tile(SW_WIDTH), tile(RK_WIDTH), tile(PLE_DIM)]
                 + [_vmem_spec()] * 10,
        out_specs=tile(D_MODEL),
        out_shape=jax.ShapeDtypeStruct((SEQ, D_MODEL), F32),
        scratch_shapes=[pltpu.VMEM((T, D_MODEL), BF16),
                        pltpu.VMEM((8, 2 * D_FF), F32), pltpu.VMEM((T, D_FF), BF16)],
        compiler_params=_params(("arbitrary",)),
        name="outproj_ffn_ple",
    )(h, o_a, o_b, o_c, p_l, prm["w_out"], prm["g_ffn"], prm["w_up"], prm["conv_w"], prm["conv_b"],
      prm["w_down"], prm["g_ple"], prm["w_gate"], prm["w_ple"], prm["g_final"])


def _pad_rows(w, start, total):
    out = jnp.zeros((total, w.shape[1]), w.dtype)
    return lax.dynamic_update_slice(out, w, (start, 0))


def kernel(x, p, w_in, w_out, norm_mix_g, norm_ffn_g, norm_ple_g, final_norm_g, hgrn_lower_bounds, hgrn_gnorm_g, rwkv_mu, rwkv_w0, rwkv_w2, rwkv_a0, rwkv_a2, rwkv_g2, rwkv_k_k, rwkv_k_a, rwkv_r_k, rwkv_ln_w, rwkv_ln_b, rwkv_v0, rwkv_v1, rwkv_v2, ffn_up, ffn_conv_w, ffn_conv_b, ffn_down, ple_proj, ple_gate):
    row = lambda a: a.reshape(1, -1)
    h = x.reshape(SEQ, D_MODEL)
    v_first = None
    rwkv_w = dict(rwkv_mu=rwkv_mu, rwkv_w0=rwkv_w0, rwkv_w2=rwkv_w2, rwkv_a0=rwkv_a0, rwkv_a2=rwkv_a2,
                  rwkv_g2=rwkv_g2, rwkv_k_k=rwkv_k_k, rwkv_k_a=rwkv_k_a, rwkv_r_k=rwkv_r_k,
                  rwkv_ln_w=rwkv_ln_w, rwkv_ln_b=rwkv_ln_b, rwkv_v0=rwkv_v0, rwkv_v1=rwkv_v1, rwkv_v2=rwkv_v2)
    for l in range(DEPTH):
        z = _inproj(h, row(norm_mix_g[l]), w_in[l].astype(BF16))
        o_a = _hgrn(z, hgrn_lower_bounds, row(hgrn_gnorm_g[l]), l)
        o_b = _dilated_mix(z)
        o_c, v_first = _rwkv(z, _rwkv_params(rwkv_w, l), v_first)
        post = {
            "w_out": w_out[l].astype(BF16), "g_ffn": row(norm_ffn_g[l]),
            "w_up": ffn_up[l].astype(BF16), "conv_w": ffn_conv_w[l], "conv_b": row(ffn_conv_b[l]),
            "w_down": ffn_down[l].astype(BF16), "g_ple": row(norm_ple_g[l]),
            "w_gate": ple_gate[l].astype(BF16), "w_ple": ple_proj[l].astype(BF16),
            "g_final": row(final_norm_g),
        }
        h = _post(h, o_a, o_b, o_c, p[l].reshape(SEQ, PLE_DIM), post, final=l == DEPTH - 1)
    return h.reshape(1, SEQ, D_MODEL)
```

```python
import functools

import numpy as np
import jax
import jax.numpy as jnp
from jax import lax
from jax.experimental import pallas as pl
from jax.experimental.pallas import tpu as pltpu

F32 = jnp.float32
BF16 = jnp.bfloat16

D_MODEL = 1024
SEQ = 16384
DEPTH = 2
HG_HEADS = 4
HG_DIM = 128
HG_WIDTH = 512
SW_HEADS = 4
SW_HDIM = 64
SW_WIDTH = 256
SW_PATTERNS = ((128, 1), (512, 4), (2048, 16))
SW_BLOCK = 128
RK_HEADS = 4
RK_HDIM = 64
RK_WIDTH = 256
RK_GN_EPS = 64e-5
RK_NORM_EPS = 1e-12
N_IN = 3840
D_FF = 2816
PLE_DIM = 256
NORM_EPS = 1e-6

LANES = 128
MXU_DIM = 256
VMEM_LIMIT_BYTES = 56 * 1024 * 1024

ROW_TILE = 512
FFN_COLS = 256
HG_TILE = 256
HG_CHUNK = 64
HG_SAFE_EXP = 80.0
AT_SPAN = SW_BLOCK * max(d for _, d in SW_PATTERNS)
ATTN_MASKED = float("-inf")
AT_GROUP = 2
RK_CHUNK = 64
RK_TILE = 512
RK_LEVELS = 6
RK_CUM = MXU_DIM

NN = (((1,), (0,)), ((), ()))
NT = (((1,), (1,)), ((), ()))


def _dot(a, b, dims=NN):
    return lax.dot_general(a, b, dims, preferred_element_type=F32)


def _mm(a, b, dims=NN):
    return _dot(a.astype(BF16), b.astype(BF16), dims)


def _split3(x):
    x1 = x.astype(BF16)
    r1 = x - x1.astype(F32)
    x2 = r1.astype(BF16)
    x3 = (r1 - x2.astype(F32)).astype(BF16)
    return x1, x2, x3


def _sel_left(m01, x, terms=3):
    x1, x2, x3 = _split3(x)
    out = _dot(m01, x1) + _dot(m01, x2)
    return out + _dot(m01, x3) if terms == 3 else out


def _sel_right(x, m01, terms=3):
    x1, x2, x3 = _split3(x)
    out = _dot(x1, m01) + _dot(x2, m01)
    return out + _dot(x3, m01) if terms == 3 else out


def _sigmoid(x):
    return 1.0 / (1.0 + jnp.exp(-x))


def _silu(x):
    return x * _sigmoid(x)


def _rms(x, g):
    return x * lax.rsqrt(jnp.mean(x * x, axis=-1, keepdims=True) + NORM_EPS) * g


def _vmem_spec():
    return pl.BlockSpec(memory_space=pltpu.VMEM)


def _params(sem):
    return pltpu.CompilerParams(dimension_semantics=sem, vmem_limit_bytes=VMEM_LIMIT_BYTES)


def _inproj_kernel(h_ref, g_ref, w_ref, z_ref):
    y = _rms(h_ref[...], g_ref[...])
    z_ref[...] = _dot(y.astype(BF16), w_ref[...])


def _inproj(h, g, w_bf):
    return pl.pallas_call(
        _inproj_kernel,
        grid=(SEQ // ROW_TILE,),
        in_specs=[pl.BlockSpec((ROW_TILE, D_MODEL), lambda i: (i, 0)),
                  _vmem_spec(), _vmem_spec()],
        out_specs=pl.BlockSpec((ROW_TILE, N_IN), lambda i: (i, 0)),
        out_shape=jax.ShapeDtypeStruct((SEQ, N_IN), F32),
        compiler_params=_params(("arbitrary",)),
        name="inproj",
    )(h, g, w_bf)


def _hgrn_kernel(q_ref, f_ref, i_ref, g_ref, lbp_ref, gn_ref, tri_ref, o_ref,
                 st_s, b_s, qq_s, kk_s, oi_s, *, layer):
    C, T, D = HG_CHUNK, HG_TILE, HG_DIM

    @pl.when(pl.program_id(0) == 0)
    def _():
        st_s[...] = jnp.zeros_like(st_s)

    lbp = lbp_ref[...]
    e = jnp.exp(lbp - jnp.max(lbp, axis=0, keepdims=True))
    sm = e / jnp.sum(e, axis=0, keepdims=True)
    lb = jnp.zeros((1, HG_WIDTH), F32)
    for j in range(1, layer + 1):
        lb = lb + sm[j:j + 1, :]

    q = _silu(q_ref[...])
    f = lb + (1.0 - lb) * _sigmoid(f_ref[...])
    kk = 1.0 - f
    lf = jnp.log(f)
    b = _sel_left(tri_ref[...], lf, terms=2)
    v = i_ref[...]
    b_s[...] = b
    qq_s[...] = q
    kk_s[...] = kk
    safe = jnp.min(b) > -HG_SAFE_EXP
    qe = (q * jnp.exp(b)).astype(BF16)
    kt = (kk * jnp.exp(-jnp.maximum(b, -HG_SAFE_EXP))).astype(BF16)
    vb = v.astype(BF16)
    half = lax.broadcasted_iota(jnp.int32, (2 * C, D), 0) < C
    causal = (lax.broadcasted_iota(jnp.int32, (C, C), 1) <= lax.broadcasted_iota(jnp.int32, (C, C), 0))

    heads = [slice(h * D, (h + 1) * D) for h in range(HG_HEADS)]
    units = [(n, h) for n in range(T // C) for h in range(HG_HEADS)]
    rows = [slice(n * C, (n + 1) * C) for n in range(T // C)]
    vts = [v[:, lanes].T.astype(BF16) for lanes in heads]
    tot = {(n, h): b[(n + 1) * C - 1:(n + 1) * C, heads[h]] for n, h in units}
    scores = {(n, h): _dot(qe[rows[n], heads[h]], kt[rows[n], heads[h]], NT) for n, h in units}
    intra = {(n, h): _dot(jnp.where(causal, scores[n, h], 0.0).astype(BF16), vb[rows[n], heads[h]])
             for n, h in units}
    upd = {}
    for n, h in units:
        slab = slice((n // 2) * 2 * C, (n // 2 + 1) * 2 * C)
        first = half if n % 2 == 0 else jnp.logical_not(half)
        ke = jnp.where(first, kk[slab, heads[h]] * jnp.exp(tot[n, h] - b[slab, heads[h]]), 0.0)
        upd[n, h] = _dot(vts[h][:, slab], ke.astype(BF16))
    state = {(0, h): st_s[h] for h in range(HG_HEADS)}
    for n, h in units:
        state[n + 1, h] = state[n, h] * jnp.exp(tot[n, h]) + upd[n, h]
    for h in range(HG_HEADS):
        st_s[h] = state[T // C, h]
    for n, h in units:
        inter = _dot(qe[rows[n], heads[h]], state[n, h].astype(BF16), NT)
        oi_s[rows[n], heads[h]] = inter
        o_ref[rows[n], heads[h]] = inter + intra[n, h]

    @pl.when(jnp.logical_not(safe))
    def _():
        rid = lax.broadcasted_iota(jnp.int32, (C, D), 0)

        def pick(x, i):
            return jnp.sum(jnp.where(rid == i, x, 0.0), axis=0, keepdims=True)

        for h in range(HG_HEADS):
            lanes = slice(h * D, (h + 1) * D)
            for n in range(T // C):
                rows = slice(n * C, (n + 1) * C)
                b_n, q_n, k_n, v_n = b_s[rows, lanes], qq_s[rows, lanes], kk_s[rows, lanes], i_ref[rows, lanes]

                def row(i, od, b_n=b_n, q_n=q_n, k_n=k_n, v_n=v_n):
                    w = jnp.exp(pick(b_n, i) - b_n) * (pick(q_n, i) * k_n)
                    s = jnp.sum(jnp.where(rid <= i, w, 0.0), axis=-1, keepdims=True)
                    return jnp.where(rid == i, jnp.sum(s * v_n, axis=0, keepdims=True), od)

                o_ref[rows, lanes] = oi_s[rows, lanes] + lax.fori_loop(0, C, row, jnp.zeros((C, D), F32))

    gate = gn_ref[...] * _silu(g_ref[...])
    for h in range(HG_HEADS):
        lanes = slice(h * D, (h + 1) * D)
        o = o_ref[:, lanes]
        o = o * lax.rsqrt(jnp.mean(o * o, axis=-1, keepdims=True) + NORM_EPS)
        o_ref[:, lanes] = o * gate[:, lanes]


def _hgrn(z, lb_params, gnorm_g, layer):
    T, C = HG_TILE, HG_CHUNK
    assert HG_DIM == LANES and T % (2 * C) == 0
    t = np.arange(T)
    tri = jnp.asarray(((t[:, None] // C) == (t[None, :] // C)) & (t[None, :] <= t[:, None]), BF16)
    col = lambda j: pl.BlockSpec((T, HG_WIDTH), lambda i: (i, j))
    scr = pltpu.VMEM((T, HG_WIDTH), F32)
    return pl.pallas_call(
        functools.partial(_hgrn_kernel, layer=layer),
        grid=(SEQ // T,),
        in_specs=[col(0), col(1), col(2), col(3), _vmem_spec(), _vmem_spec(), _vmem_spec()],
        out_specs=pl.BlockSpec((T, HG_WIDTH), lambda i: (i, 0)),
        out_shape=jax.ShapeDtypeStruct((SEQ, HG_WIDTH), F32),
        scratch_shapes=[pltpu.VMEM((HG_HEADS, HG_DIM, HG_DIM), F32), scr, scr, scr, scr],
        compiler_params=_params(("arbitrary",)),
        name="hgrn2",
    )(z, z, z, z, lb_params, gnorm_g, tri)


def _attn_kernel(*refs):
    nh = SW_WIDTH // LANES
    q_ref, ko_ref, vo_ref, kp_ref, vp_ref = (refs[j * nh:(j + 1) * nh] for j in range(5))
    hm_ref, o_ref = refs[5 * nh], refs[5 * nh + 1]
    acc_s, m_s, l_s = (refs[5 * nh + 2 + j * nh:5 * nh + 2 + (j + 1) * nh] for j in range(3))

    def ld(halves, rows):
        return jnp.concatenate([h[rows, :] for h in halves], axis=1)

    def st(halves, rows, val):
        for j, h in enumerate(halves):
            h[rows, :] = val[:, j * LANES:(j + 1) * LANES]

    B = SW_BLOCK
    i = pl.program_id(0)
    hm = hm_ref[...].astype(BF16)
    shape = (SW_HEADS * B, 2 * B)
    qi = lax.broadcasted_iota(jnp.int32, shape, 0) & (B - 1)
    kj = lax.broadcasted_iota(jnp.int32, shape, 1)
    band_bias = jnp.where((kj >= qi) & (kj <= qi + B), 0.0, ATTN_MASKED)
    prev_bias = jnp.where(lax.broadcasted_iota(jnp.int32, (1, 2 * B), 1) < B, ATTN_MASKED, 0.0)
    lane_head = lax.broadcasted_iota(jnp.int32, (B, SW_WIDTH), 1) // SW_HDIM

    for pidx, (window, dil) in enumerate(SW_PATTERNS):
        per_res = AT_SPAN // (dil * B)

        def rows_at(s, dil=dil):
            if dil == 1:
                return pl.ds(pl.multiple_of(s, B), B)
            return pl.ds(s, B, stride=dil)

        def scores(u, dil=dil, per_res=per_res):
            r = u // per_res
            mb = u % per_res
            start = r + dil * B * mb
            rows = rows_at(start)
            q = ld(q_ref, rows) * (SW_HDIM ** -0.5)
            prow = pl.ds(AT_SPAN - dil * B + r, B, stride=dil) if dil > 1 else pl.ds(AT_SPAN - B, B)
            k_prev = ld(kp_ref, prow)
            v_prev = ld(vp_ref, prow)
            if per_res > 1:
                inner = mb > 0
                srows = rows_at(jnp.where(inner, start - dil * B, start))
                k_prev = jnp.where(inner, ld(ko_ref, srows), k_prev)
                v_prev = jnp.where(inner, ld(vo_ref, srows), v_prev)
                has_prev = inner | (i > 0)
            else:
                has_prev = i > 0
            kw = jnp.concatenate([k_prev, ld(ko_ref, rows)], axis=0).astype(BF16)
            vw = jnp.concatenate([v_prev, ld(vo_ref, rows)], axis=0).astype(BF16)
            q4 = jnp.concatenate([q.astype(BF16)] * SW_HEADS, axis=0) * hm
            s = _dot(q4, kw, NT) + band_bias + jnp.where(has_prev, 0.0, prev_bias)
            return rows, s, vw

        def softmax(s):
            m = jnp.max(s, axis=-1, keepdims=True)
            pe = jnp.exp(s - m)
            return m, pe.astype(BF16), jnp.sum(pe, axis=-1, keepdims=True)

        def merge(rows, m, l, pv, pidx=pidx):
            acc = pv[0:B, :]
            mf = jnp.broadcast_to(m[0:B, :], (B, SW_WIDTH))
            lf = jnp.broadcast_to(l[0:B, :], (B, SW_WIDTH))
            for h in range(1, SW_HEADS):
                hs = slice(h * B, (h + 1) * B)
                sel = lane_head == h
                acc = jnp.where(sel, pv[hs, :], acc)
                mf = jnp.where(sel, m[hs, :], mf)
                lf = jnp.where(sel, l[hs, :], lf)
            if pidx > 0:
                m_old = ld(m_s, rows)
                m_new = jnp.maximum(m_old, mf)
                a_old = jnp.exp(m_old - m_new)
                a_cur = jnp.exp(mf - m_new)
                acc = ld(acc_s, rows) * a_old + acc * a_cur
                lf = ld(l_s, rows) * a_old + lf * a_cur
                mf = m_new
            return acc, mf, lf

        def group(gi, carry):
            units = [gi * AT_GROUP + t for t in range(AT_GROUP)]
            sc = [scores(u) for u in units]
            sm = [softmax(s) for _, s, _ in sc]
            pv = [_dot(pe, vw) for (_, pe, _), (_, _, vw) in zip(sm, sc)]
            out = [merge(rows, m, l, p) for (rows, _, _), (m, _, l), p in zip(sc, sm, pv)]
            for (rows, _, _), (acc, mf, lf) in zip(sc, out):
                st(acc_s, rows, acc)
                st(m_s, rows, mf)
                st(l_s, rows, lf)
            return carry

        lax.fori_loop(0, AT_SPAN // (B * AT_GROUP), group, 0)

    for j in range(nh):
        o_ref[:, j * LANES:(j + 1) * LANES] = acc_s[j][...] / l_s[j][...]


def _dilated_mix(z):
    for window, dil in SW_PATTERNS:
        assert window // dil == SW_BLOCK and AT_SPAN % (dil * SW_BLOCK) == 0
    nh = SW_WIDTH // LANES
    qcol = (HG_HEADS * 4 * HG_DIM) // LANES
    own = lambda c: pl.BlockSpec((AT_SPAN, LANES), lambda i: (i, qcol + c))
    prev = lambda c: pl.BlockSpec((AT_SPAN, LANES), lambda i: (jnp.maximum(i - 1, 0), qcol + c))
    row_head = np.arange(SW_HEADS * SW_BLOCK) // SW_BLOCK
    lane_head = np.arange(SW_WIDTH) // SW_HDIM
    hm = jnp.asarray(row_head[:, None] == lane_head[None, :], F32)
    scr = pltpu.VMEM((AT_SPAN, LANES), F32)
    return pl.pallas_call(
        _attn_kernel,
        grid=(SEQ // AT_SPAN,),
        in_specs=[own(c) for c in range(3 * nh)] + [prev(c) for c in range(nh, 3 * nh)] + [_vmem_spec()],
        out_specs=pl.BlockSpec((AT_SPAN, SW_WIDTH), lambda i: (i, 0)),
        out_shape=jax.ShapeDtypeStruct((SEQ, SW_WIDTH), F32),
        scratch_shapes=[scr] * (3 * nh),
        compiler_params=_params(("arbitrary",)),
        name="dilated_attn",
    )(*([z] * (5 * nh)), hm)


def _rwkv_masks():
    t = np.arange(RK_CHUNK)[:, None]
    s = np.tile(np.arange(RK_CHUNK), RK_HEADS)[None, :]
    masks = [s < t, s <= t, s == t]
    for lv in range(RK_LEVELS):
        masks.append(((t >> (lv + 1)) == (s >> (lv + 1))) & (((t >> lv) & 1) == 1) & (((s >> lv) & 1) == 0))
    key_eye = np.arange(RK_HDIM)[:, None] == np.tile(np.arange(RK_HDIM), RK_HEADS)[None, :]
    return np.stack(masks).astype(np.float32), key_eye.astype(np.float32)


def _head_mask(row_block, col_block):
    rh = np.arange(RK_HEADS * row_block) // row_block
    ch = np.arange(RK_HEADS * col_block) // col_block
    return rh[:, None] == ch[None, :]


M_STRICT, M_LOWER, M_EYE, M_LEVEL0 = 0, 1, 2, 3


def _rwkv_kernel(*refs, has_vres):
    (zr_ref, zk_ref, zv_ref, zl_ref, mu_ref, w0_ref, a0_ref, w2_ref, a2_ref, g2_ref,
     kkw_ref, kaw_ref, rkw_ref, lnw_ref, lnb_ref, bd_ref, tri_ref, msk_ref, keye_ref,
     hmn_ref, hmp_ref) = refs[:21]
    refs = refs[21:]
    if has_vres:
        vf_ref, v0_ref, v1_ref, v2_ref = refs[:4]
        refs = refs[4:]
        y_ref, st_s, cb_s = refs
    else:
        y_ref, vfo_ref, st_s, cb_s = refs
    C, T, HD = RK_CHUNK, RK_TILE, RK_HDIM

    @pl.when(pl.program_id(0) == 0)
    def _():
        st_s[...] = jnp.zeros_like(st_s)
        cb_s[...] = jnp.zeros_like(cb_s)

    def shift_mix(idx, ref):
        c = ref[...]
        cb_s[idx, 8:8 + T, :] = c
        cp = cb_s[idx, 7:7 + T, :]
        cb_s[idx, 0:8, :] = c[T - 8:T, :]
        return c + (cp - c) * mu_ref[idx:idx + 1, :]

    r = shift_mix(0, zr_ref)
    k0 = shift_mix(1, zk_ref)
    v = shift_mix(2, zv_ref)
    lo = shift_mix(3, zl_ref)

    bd = bd_ref[...]
    wpre = w0_ref[...] + _mm(jnp.tanh(lo), w2_ref[...])
    sp = jnp.maximum(-wpre, 0.0) + jnp.log(1.0 + jnp.exp(-jnp.abs(wpre)))
    logw = -jnp.exp(-sp - 0.5)
    a = _sigmoid(a0_ref[...] + _mm(lo, a2_ref[...]))
    g = _mm(_sigmoid(lo), g2_ref[...])
    if has_vres:
        mix = _sigmoid(v0_ref[...] + _mm(_mm(v, v1_ref[...]), v2_ref[...]))
        v = v + (vf_ref[...] - v) * mix
    else:
        vfo_ref[...] = v
    kk = k0 * kkw_ref[...]
    kk = kk * lax.rsqrt(jnp.maximum(_sel_right(kk * kk, bd, terms=2), RK_NORM_EPS * RK_NORM_EPS))
    k = k0 * (1.0 + (a - 1.0) * kaw_ref[...])
    beta = kk * a

    lw = jnp.concatenate([_sel_left(tri_ref[...], logw[i:i + RK_CUM, :], terms=2)
                          for i in range(0, T, RK_CUM)], axis=0)
    e_neg = jnp.exp(-lw)
    kap = kk * jnp.exp(lw - logw)
    bh = beta * e_neg
    kh = k * e_neg
    rh = r * jnp.exp(lw)
    hmn = hmn_ref[...]
    hmp = hmp_ref[...]
    hmn_f = hmn.astype(F32)

    def stack_n(x):
        return jnp.concatenate([x.astype(BF16)] * RK_HEADS, axis=0) * hmn

    def stack_p(x):
        return jnp.concatenate([x.astype(BF16)] * RK_HEADS, axis=0) * hmp

    def stack_s(x):
        return jnp.concatenate([x.astype(BF16)] * RK_HEADS, axis=0) * bd

    def packed_t(x):
        full = (jnp.concatenate([x] * RK_HEADS, axis=0) * hmn_f).T
        out = full[0:HD, :]
        for h in range(1, RK_HEADS):
            out = out + full[h * HD:(h + 1) * HD, :]
        return out

    def mm(x, w, dims=NN):
        return _dot(x.astype(BF16), w, dims)

    chunks = range(T // C)
    rows = [slice(c * C, (c + 1) * C) for c in chunks]
    each = lambda fn, *lists: [fn(*args) for args in zip(*lists)]
    lwc = [lw[(c + 1) * C - 1:(c + 1) * C, :] for c in chunks]
    e_end = [jnp.exp(lwc[c] - lw[rows[c], :]) for c in chunks]
    kap_c = [kap[rw, :] for rw in rows]
    rh_c = [rh[rw, :] for rw in rows]
    bh4 = [stack_n(bh[rw, :]) for rw in rows]
    kh4 = [stack_n(kh[rw, :]) for rw in rows]
    v4 = [stack_n(v[rw, :]) for rw in rows]
    rcat = lambda *xs: jnp.concatenate(xs, axis=0)
    kap_rh = each(rcat, kap_c, rh_c)
    with_b = each(lambda x, w: mm(x, w, NT), kap_rh, bh4)
    with_k = each(lambda x, w: mm(x, w, NT), kap_rh, kh4)
    g_kb = [t[0:C, :] for t in with_b]
    a_rb = [t[C:2 * C, :] * msk_ref[M_LOWER] for t in with_b]
    a_k = [t[0:C, :] * msk_ref[M_STRICT] for t in with_k]
    a_rk = [t[C:2 * C, :] * msk_ref[M_LOWER] for t in with_k]
    x = [msk_ref[M_EYE] - gc * msk_ref[M_LEVEL0] for gc in g_kb]
    for lv in range(1, RK_LEVELS):
        xl = each(lambda xc, gc: mm(xc, stack_p(gc * msk_ref[M_LEVEL0 + lv])), x, g_kb)
        x = each(lambda xc, xlc: xc - mm(xlc, stack_p(xc)), x, xl)
    m1 = each(lambda xc, kc: mm(xc, stack_n(kc)), x, kap_c)
    bwt = [packed_t(beta[rows[c], :] * e_end[c]) for c in chunks]
    kwt = [packed_t(k[rows[c], :] * e_end[c]) for c in chunks]
    with_v = each(lambda ak, ark, kw, vc: mm(rcat(ak, ark, kw), vc), a_k, a_rk, kwt, v4)
    m2 = each(lambda xc, t: mm(xc, stack_n(t[0:C, :])), x, with_v)
    arb_bwt = each(rcat, a_rb, bwt)
    with_m1 = each(lambda l, mc: mm(l, stack_n(mc)), arb_bwt, m1)
    with_m2 = each(lambda l, mc: mm(l, stack_n(mc)), arb_bwt, m2)
    q1 = each(lambda rc, t: rc - t[0:C, :], rh_c, with_m1)
    q2 = each(lambda tv, t: tv[C:2 * C, :] - t[0:C, :], with_v, with_m2)
    pt = each(lambda lc, t: keye_ref[...] * jnp.exp(lc) - t[C:C + HD, :], lwc, with_m1)
    rt = each(lambda tv, t: tv[2 * C:2 * C + HD, :] - t[C:C + HD, :], with_v, with_m2)

    st = st_s[...]
    ys = []
    for c in chunks:
        with_st = mm(rcat(q1[c], pt[c]), stack_s(st))
        ys.append(with_st[0:C, :] + q2[c])
        st = with_st[C:C + HD, :] + rt[c]
    st_s[...] = st
    y = jnp.concatenate(ys, axis=0)

    inv_n = 1.0 / RK_HDIM
    mean = _sel_right(y, bd, terms=2) * inv_n
    d = y - mean
    var = _sel_right(d * d, bd, terms=2) * inv_n
    yn = d * lax.rsqrt(var + RK_GN_EPS) * lnw_ref[...] + lnb_ref[...]
    bonus = _sel_right(r * k * rkw_ref[...], bd, terms=2) * v
    y_ref[...] = (yn + bonus) * g


def _rwkv_params(w, l):
    row = lambda a: a.reshape(1, -1)
    n_dec, n_a = w["rwkv_w2"].shape[1], w["rwkv_a2"].shape[1]
    prm = {
        "mu": w["rwkv_mu"][l].reshape(4, RK_WIDTH),
        "w0": row(w["rwkv_w0"][l]), "a0": row(w["rwkv_a0"][l]),
        "w2": _pad_rows(w["rwkv_w2"][l], 0, RK_WIDTH).astype(BF16),
        "a2": _pad_rows(w["rwkv_a2"][l], n_dec, RK_WIDTH).astype(BF16),
        "g2": _pad_rows(w["rwkv_g2"][l], n_dec + n_a, RK_WIDTH).astype(BF16),
        "k_k": row(w["rwkv_k_k"][l]), "k_a": row(w["rwkv_k_a"][l]), "r_k": row(w["rwkv_r_k"][l]),
        "ln_w": row(w["rwkv_ln_w"][l]), "ln_b": row(w["rwkv_ln_b"][l]),
    }
    if l > 0:
        v1, v2 = w["rwkv_v1"][l - 1], w["rwkv_v2"][l - 1]
        prm["v0"] = row(w["rwkv_v0"][l - 1])
        prm["v1"] = jnp.pad(v1, ((0, 0), (0, LANES - v1.shape[1]))).astype(BF16)
        prm["v2"] = _pad_rows(v2, 0, LANES).astype(BF16)
    return prm


def _rwkv(z, prm, v_first):
    has_vres = v_first is not None
    C, T = RK_CHUNK, RK_TILE
    assert (1 << RK_LEVELS) == C and T % C == 0 and RK_HEADS * RK_HDIM == RK_WIDTH
    ccol = (HG_HEADS * 4 * HG_DIM + 3 * SW_WIDTH) // RK_WIDTH
    zcol = lambda off: pl.BlockSpec((T, RK_WIDTH), lambda i: (i, ccol + off))
    row = pl.BlockSpec((T, RK_WIDTH), lambda i: (i, 0))
    lane = np.arange(RK_WIDTH)
    bd = jnp.asarray((lane[:, None] // RK_HDIM) == (lane[None, :] // RK_HDIM), BF16)
    t = np.arange(RK_CUM)
    tri = jnp.asarray(((t[:, None] // C) == (t[None, :] // C)) & (t[None, :] <= t[:, None]), BF16)
    packed_masks, key_eye = _rwkv_masks()
    args = [z, z, z, z, prm["mu"], prm["w0"], prm["a0"], prm["w2"], prm["a2"], prm["g2"],
            prm["k_k"], prm["k_a"], prm["r_k"], prm["ln_w"], prm["ln_b"], bd, tri,
            jnp.asarray(packed_masks), jnp.asarray(key_eye),
            jnp.asarray(_head_mask(C, RK_HDIM), BF16), jnp.asarray(_head_mask(C, C), BF16)]
    in_specs = [zcol(0), zcol(1), zcol(2), zcol(3)] + [_vmem_spec()] * 17
    if has_vres:
        args += [v_first, prm["v0"], prm["v1"], prm["v2"]]
        in_specs += [row] + [_vmem_spec()] * 3
        out_specs = row
        out_shape = jax.ShapeDtypeStruct((SEQ, RK_WIDTH), F32)
    else:
        out_specs = [row, row]
        out_shape = [jax.ShapeDtypeStruct((SEQ, RK_WIDTH), F32)] * 2
    out = pl.pallas_call(
        functools.partial(_rwkv_kernel, has_vres=has_vres),
        grid=(SEQ // T,),
        in_specs=in_specs,
        out_specs=out_specs,
        out_shape=out_shape,
        scratch_shapes=[pltpu.VMEM((RK_HDIM, RK_WIDTH), F32), pltpu.VMEM((4, T + 8, RK_WIDTH), F32)],
        compiler_params=_params(("arbitrary",)),
        name="rwkv7",
    )(*args)
    if has_vres:
        return out, v_first
    return out[0], out[1]


def _post_kernel(h_ref, oa_ref, ob_ref, oc_ref, p_ref, wout_ref, gffn_ref, wup_ref, cw_ref, cb_ref,
                 wdn_ref, gple_ref, wg_ref, wp_ref, gfin_ref, out_ref,
                 hn_s, carry_s, act_s, *, final):
    T = ROW_TILE

    @pl.when(pl.program_id(0) == 0)
    def _():
        carry_s[...] = jnp.zeros_like(carry_s)

    mix = (_dot(oa_ref[...].astype(BF16), wout_ref[0:HG_WIDTH, :])
           + _dot(ob_ref[...].astype(BF16), wout_ref[HG_WIDTH:HG_WIDTH + SW_WIDTH, :])
           + _dot(oc_ref[...].astype(BF16), wout_ref[HG_WIDTH + SW_WIDTH:, :]))
    h1 = h_ref[...] + mix
    hn_s[...] = _rms(h1, gffn_ref[...]).astype(BF16)

    rid = lax.broadcasted_iota(jnp.int32, (8, FFN_COLS), 0)

    def up(col):
        return _dot(hn_s[...], wup_ref[:, pl.ds(col, FFN_COLS)])

    def conv(col, u):
        cols = pl.ds(col, FFN_COLS)
        prev = carry_s[:, cols]
        carry_s[:, cols] = u[T - 8:T, :]
        r1 = pltpu.roll(u, 1, 0)
        r2 = pltpu.roll(u, 2, 0)
        top1 = jnp.where(rid == 0, prev[7:8, :], r1[0:8, :])
        top2 = jnp.where(rid == 0, prev[6:7, :], jnp.where(rid == 1, prev[7:8, :], r2[0:8, :]))
        u1 = jnp.concatenate([top1, r1[8:, :]], axis=0)
        u2 = jnp.concatenate([top2, r2[8:, :]], axis=0)
        return cb_ref[:, cols] + cw_ref[0:1, cols] * u2 + cw_ref[1:2, cols] * u1 + cw_ref[2:3, cols] * u

    def up_pair(j):
        return up(j * FFN_COLS), up(j * FFN_COLS + D_FF)

    n_steps = D_FF // FFN_COLS
    u_gate, u_val = up_pair(0)
    for j in range(n_steps):
        nxt = up_pair(j + 1) if j + 1 < n_steps else None
        gate = conv(j * FFN_COLS, u_gate)
        val = conv(j * FFN_COLS + D_FF, u_val)
        act_s[:, pl.ds(j * FFN_COLS, FFN_COLS)] = (_silu(gate) * val).astype(BF16)
        if nxt is not None:
            u_gate, u_val = nxt

    h2 = h1 + _dot(act_s[...], wdn_ref[...])
    gate = _sigmoid(_dot(_rms(h2, gple_ref[...]).astype(BF16), wg_ref[...]))
    pe = _dot(p_ref[...].astype(BF16), wp_ref[...])
    h3 = h2 + gate * pe
    out_ref[...] = _rms(h3, gfin_ref[...]) if final else h3


def _post(h, o_a, o_b, o_c, p_l, prm, final):
    T = ROW_TILE
    tile = lambda w: pl.BlockSpec((T, w), lambda i: (i, 0))
    return pl.pallas_call(
        functools.partial(_post_kernel, final=final),
        grid=(SEQ // T,),
        in_specs=[tile(D_MODEL), tile(HG_WIDTH), tile(SW_WIDTH), tile(RK_WIDTH), tile(PLE_DIM)]
                 + [_vmem_spec()] * 10,
        out_specs=tile(D_MODEL),
        out_shape=jax.ShapeDtypeStruct((SEQ, D_MODEL), F32),
        scratch_shapes=[pltpu.VMEM((T, D_MODEL), BF16),
                        pltpu.VMEM((8, 2 * D_FF), F32), pltpu.VMEM((T, D_FF), BF16)],
        compiler_params=_params(("arbitrary",)),
        name="outproj_ffn_ple",
    )(h, o_a, o_b, o_c, p_l, prm["w_out"], prm["g_ffn"], prm["w_up"], prm["conv_w"], prm["conv_b"],
      prm["w_down"], prm["g_ple"], prm["w_gate"], prm["w_ple"], prm["g_final"])


def _pad_rows(w, start, total):
    out = jnp.zeros((total, w.shape[1]), w.dtype)
    return lax.dynamic_update_slice(out, w, (start, 0))


def kernel(x, p, w_in, w_out, norm_mix_g, norm_ffn_g, norm_ple_g, final_norm_g, hgrn_lower_bounds, hgrn_gnorm_g, rwkv_mu, rwkv_w0, rwkv_w2, rwkv_a0, rwkv_a2, rwkv_g2, rwkv_k_k, rwkv_k_a, rwkv_r_k, rwkv_ln_w, rwkv_ln_b, rwkv_v0, rwkv_v1, rwkv_v2, ffn_up, ffn_conv_w, ffn_conv_b, ffn_down, ple_proj, ple_gate):
    row = lambda a: a.reshape(1, -1)
    h = x.reshape(SEQ, D_MODEL)
    v_first = None
    rwkv_w = dict(rwkv_mu=rwkv_mu, rwkv_w0=rwkv_w0, rwkv_w2=rwkv_w2, rwkv_a0=rwkv_a0, rwkv_a2=rwkv_a2,
                  rwkv_g2=rwkv_g2, rwkv_k_k=rwkv_k_k, rwkv_k_a=rwkv_k_a, rwkv_r_k=rwkv_r_k,
                  rwkv_ln_w=rwkv_ln_w, rwkv_ln_b=rwkv_ln_b, rwkv_v0=rwkv_v0, rwkv_v1=rwkv_v1, rwkv_v2=rwkv_v2)
    for l in range(DEPTH):
        z = _inproj(h, row(norm_mix_g[l]), w_in[l].astype(BF16))
        o_a = _hgrn(z, hgrn_lower_bounds, row(hgrn_gnorm_g[l]), l)
        o_b = _dilated_mix(z)
        o_c, v_first = _rwkv(z, _rwkv_params(rwkv_w, l), v_first)
        post = {
            "w_out": w_out[l].astype(BF16), "g_ffn": row(norm_ffn_g[l]),
            "w_up": ffn_up[l].astype(BF16), "conv_w": ffn_conv_w[l], "conv_b": row(ffn_conv_b[l]),
            "w_down": ffn_down[l].astype(BF16), "g_ple": row(norm_ple_g[l]),
            "w_gate": ple_gate[l].astype(BF16), "w_ple": ple_proj[l].astype(BF16),
            "g_final": row(final_norm_g),
        }
        h = _post(h, o_a, o_b, o_c, p[l].reshape(SEQ, PLE_DIM), post, final=l == DEPTH - 1)
    return h.reshape(1, SEQ, D_MODEL)
```

```python
import functools

import numpy as np
import jax
import jax.numpy as jnp
from jax import lax
from jax.experimental import pallas as pl
from jax.experimental.pallas import tpu as pltpu

F32 = jnp.float32
BF16 = jnp.bfloat16

D_MODEL = 1024
SEQ = 16384
DEPTH = 2
HG_HEADS = 4
HG_DIM = 128
HG_WIDTH = 512
SW_HEADS = 4
SW_HDIM = 64
SW_WIDTH = 256
SW_PATTERNS = ((128, 1), (512, 4), (2048, 16))
SW_BLOCK = 128
RK_HEADS = 4
RK_HDIM = 64
RK_WIDTH = 256
RK_GN_EPS = 64e-5
RK_NORM_EPS = 1e-12
N_IN = 3840
D_FF = 2816
PLE_DIM = 256
NORM_EPS = 1e-6

LANES = 128
MXU_DIM = 256
VMEM_LIMIT_BYTES = 56 * 1024 * 1024

IN_TILE = 1024
ROW_TILE = 512
FFN_COLS = 256
HG_TILE = 512
HG_CUM = MXU_DIM
HG_CHUNK = 64
HG_SAFE_EXP = 80.0
AT_SPAN = SW_BLOCK * max(d for _, d in SW_PATTERNS)
ATTN_MASKED = float("-inf")
AT_GROUP = 2
RK_CHUNK = 64
RK_TILE = 512
RK_LEVELS = 6
RK_CUM = MXU_DIM

NN = (((1,), (0,)), ((), ()))
NT = (((1,), (1,)), ((), ()))


def _dot(a, b, dims=NN):
    return lax.dot_general(a, b, dims, preferred_element_type=F32)


def _mm(a, b, dims=NN):
    return _dot(a.astype(BF16), b.astype(BF16), dims)


def _split3(x):
    x1 = x.astype(BF16)
    r1 = x - x1.astype(F32)
    x2 = r1.astype(BF16)
    x3 = (r1 - x2.astype(F32)).astype(BF16)
    return x1, x2, x3


def _sel_left(m01, x, terms=3):
    x1, x2, x3 = _split3(x)
    out = _dot(m01, x1) + _dot(m01, x2)
    return out + _dot(m01, x3) if terms == 3 else out


def _sel_right(x, m01, terms=3):
    x1, x2, x3 = _split3(x)
    out = _dot(x1, m01) + _dot(x2, m01)
    return out + _dot(x3, m01) if terms == 3 else out


def _sigmoid(x):
    return 1.0 / (1.0 + jnp.exp(-x))


def _silu(x):
    return x * _sigmoid(x)


def _rms(x, g):
    return x * lax.rsqrt(jnp.mean(x * x, axis=-1, keepdims=True) + NORM_EPS) * g


def _vmem_spec():
    return pl.BlockSpec(memory_space=pltpu.VMEM)


def _params(sem):
    return pltpu.CompilerParams(dimension_semantics=sem, vmem_limit_bytes=VMEM_LIMIT_BYTES)


def _inproj_kernel(h_ref, g_ref, w_ref, z_ref):
    y = _rms(h_ref[...], g_ref[...])
    z_ref[...] = _dot(y.astype(BF16), w_ref[...])


def _inproj(h, g, w_bf):
    return pl.pallas_call(
        _inproj_kernel,
        grid=(SEQ // IN_TILE,),
        in_specs=[pl.BlockSpec((IN_TILE, D_MODEL), lambda i: (i, 0)),
                  _vmem_spec(), _vmem_spec()],
        out_specs=pl.BlockSpec((IN_TILE, N_IN), lambda i: (i, 0)),
        out_shape=jax.ShapeDtypeStruct((SEQ, N_IN), F32),
        compiler_params=_params(("arbitrary",)),
        name="inproj",
    )(h, g, w_bf)


def _hgrn_kernel(q_ref, f_ref, i_ref, g_ref, lbp_ref, gn_ref, tri_ref, o_ref,
                 st_s, b_s, qq_s, kk_s, oi_s, *, layer):
    C, T, D = HG_CHUNK, HG_TILE, HG_DIM

    @pl.when(pl.program_id(0) == 0)
    def _():
        st_s[...] = jnp.zeros_like(st_s)

    lbp = lbp_ref[...]
    e = jnp.exp(lbp - jnp.max(lbp, axis=0, keepdims=True))
    sm = e / jnp.sum(e, axis=0, keepdims=True)
    lb = jnp.zeros((1, HG_WIDTH), F32)
    for j in range(1, layer + 1):
        lb = lb + sm[j:j + 1, :]

    q = _silu(q_ref[...])
    f = lb + (1.0 - lb) * _sigmoid(f_ref[...])
    kk = 1.0 - f
    lf = jnp.log(f)
    b = jnp.concatenate([_sel_left(tri_ref[...], lf[i:i + HG_CUM, :], terms=2)
                         for i in range(0, T, HG_CUM)], axis=0)
    v = i_ref[...]
    b_s[...] = b
    qq_s[...] = q
    kk_s[...] = kk
    safe = jnp.min(b) > -HG_SAFE_EXP
    qe = (q * jnp.exp(b)).astype(BF16)
    kt = (kk * jnp.exp(-jnp.maximum(b, -HG_SAFE_EXP))).astype(BF16)
    vb = v.astype(BF16)
    half = lax.broadcasted_iota(jnp.int32, (2 * C, D), 0) < C
    causal = (lax.broadcasted_iota(jnp.int32, (C, C), 1) <= lax.broadcasted_iota(jnp.int32, (C, C), 0))

    heads = [slice(h * D, (h + 1) * D) for h in range(HG_HEADS)]
    units = [(n, h) for n in range(T // C) for h in range(HG_HEADS)]
    rows = [slice(n * C, (n + 1) * C) for n in range(T // C)]
    vts = [v[:, lanes].T.astype(BF16) for lanes in heads]
    tot = {(n, h): b[(n + 1) * C - 1:(n + 1) * C, heads[h]] for n, h in units}
    scores = {(n, h): _dot(qe[rows[n], heads[h]], kt[rows[n], heads[h]], NT) for n, h in units}
    intra = {(n, h): _dot(jnp.where(causal, scores[n, h], 0.0).astype(BF16), vb[rows[n], heads[h]])
             for n, h in units}
    upd = {}
    for n, h in units:
        slab = slice((n // 2) * 2 * C, (n // 2 + 1) * 2 * C)
        first = half if n % 2 == 0 else jnp.logical_not(half)
        ke = jnp.where(first, kk[slab, heads[h]] * jnp.exp(tot[n, h] - b[slab, heads[h]]), 0.0)
        upd[n, h] = _dot(vts[h][:, slab], ke.astype(BF16))
    state = {(0, h): st_s[h] for h in range(HG_HEADS)}
    for n, h in units:
        state[n + 1, h] = state[n, h] * jnp.exp(tot[n, h]) + upd[n, h]
    for h in range(HG_HEADS):
        st_s[h] = state[T // C, h]
    for n, h in units:
        inter = _dot(qe[rows[n], heads[h]], state[n, h].astype(BF16), NT)
        oi_s[rows[n], heads[h]] = inter
        o_ref[rows[n], heads[h]] = inter + intra[n, h]

    @pl.when(jnp.logical_not(safe))
    def _():
        rid = lax.broadcasted_iota(jnp.int32, (C, D), 0)

        def pick(x, i):
            return jnp.sum(jnp.where(rid == i, x, 0.0), axis=0, keepdims=True)

        for h in range(HG_HEADS):
            lanes = slice(h * D, (h + 1) * D)
            for n in range(T // C):
                rows = slice(n * C, (n + 1) * C)
                b_n, q_n, k_n, v_n = b_s[rows, lanes], qq_s[rows, lanes], kk_s[rows, lanes], i_ref[rows, lanes]

                def row(i, od, b_n=b_n, q_n=q_n, k_n=k_n, v_n=v_n):
                    w = jnp.exp(pick(b_n, i) - b_n) * (pick(q_n, i) * k_n)
                    s = jnp.sum(jnp.where(rid <= i, w, 0.0), axis=-1, keepdims=True)
                    return jnp.where(rid == i, jnp.sum(s * v_n, axis=0, keepdims=True), od)

                o_ref[rows, lanes] = oi_s[rows, lanes] + lax.fori_loop(0, C, row, jnp.zeros((C, D), F32))

    gate = gn_ref[...] * _silu(g_ref[...])
    for h in range(HG_HEADS):
        lanes = slice(h * D, (h + 1) * D)
        o = o_ref[:, lanes]
        o = o * lax.rsqrt(jnp.mean(o * o, axis=-1, keepdims=True) + NORM_EPS)
        o_ref[:, lanes] = o * gate[:, lanes]


def _hgrn(z, lb_params, gnorm_g, layer):
    T, C = HG_TILE, HG_CHUNK
    assert HG_DIM == LANES and T % (2 * C) == 0 and T % HG_CUM == 0 and HG_CUM % C == 0
    t = np.arange(HG_CUM)
    tri = jnp.asarray(((t[:, None] // C) == (t[None, :] // C)) & (t[None, :] <= t[:, None]), BF16)
    col = lambda j: pl.BlockSpec((T, HG_WIDTH), lambda i: (i, j))
    scr = pltpu.VMEM((T, HG_WIDTH), F32)
    return pl.pallas_call(
        functools.partial(_hgrn_kernel, layer=layer),
        grid=(SEQ // T,),
        in_specs=[col(0), col(1), col(2), col(3), _vmem_spec(), _vmem_spec(), _vmem_spec()],
        out_specs=pl.BlockSpec((T, HG_WIDTH), lambda i: (i, 0)),
        out_shape=jax.ShapeDtypeStruct((SEQ, HG_WIDTH), F32),
        scratch_shapes=[pltpu.VMEM((HG_HEADS, HG_DIM, HG_DIM), F32), scr, scr, scr, scr],
        compiler_params=_params(("arbitrary",)),
        name="hgrn2",
    )(z, z, z, z, lb_params, gnorm_g, tri)


def _attn_kernel(*refs):
    nh = SW_WIDTH // LANES
    q_ref, ko_ref, vo_ref, kp_ref, vp_ref = (refs[j * nh:(j + 1) * nh] for j in range(5))
    hm_ref, o_ref = refs[5 * nh], refs[5 * nh + 1]
    acc_s, m_s, l_s = (refs[5 * nh + 2 + j * nh:5 * nh + 2 + (j + 1) * nh] for j in range(3))

    def ld(halves, rows):
        return jnp.concatenate([h[rows, :] for h in halves], axis=1)

    def st(halves, rows, val):
        for j, h in enumerate(halves):
            h[rows, :] = val[:, j * LANES:(j + 1) * LANES]

    B = SW_BLOCK
    i = pl.program_id(0)
    hm = hm_ref[...].astype(BF16)
    shape = (SW_HEADS * B, 2 * B)
    qi = lax.broadcasted_iota(jnp.int32, shape, 0) & (B - 1)
    kj = lax.broadcasted_iota(jnp.int32, shape, 1)
    band_bias = jnp.where((kj >= qi) & (kj <= qi + B), 0.0, ATTN_MASKED)
    prev_bias = jnp.where(lax.broadcasted_iota(jnp.int32, (1, 2 * B), 1) < B, ATTN_MASKED, 0.0)
    lane_head = lax.broadcasted_iota(jnp.int32, (B, SW_WIDTH), 1) // SW_HDIM

    for pidx, (window, dil) in enumerate(SW_PATTERNS):
        per_res = AT_SPAN // (dil * B)

        def rows_at(s, dil=dil):
            if dil == 1:
                return pl.ds(pl.multiple_of(s, B), B)
            return pl.ds(s, B, stride=dil)

        def scores(u, dil=dil, per_res=per_res):
            r = u // per_res
            mb = u % per_res
            start = r + dil * B * mb
            rows = rows_at(start)
            q = ld(q_ref, rows) * (SW_HDIM ** -0.5)
            prow = pl.ds(AT_SPAN - dil * B + r, B, stride=dil) if dil > 1 else pl.ds(AT_SPAN - B, B)
            k_prev = ld(kp_ref, prow)
            v_prev = ld(vp_ref, prow)
            if per_res > 1:
                inner = mb > 0
                srows = rows_at(jnp.where(inner, start - dil * B, start))
                k_prev = jnp.where(inner, ld(ko_ref, srows), k_prev)
                v_prev = jnp.where(inner, ld(vo_ref, srows), v_prev)
                has_prev = inner | (i > 0)
            else:
                has_prev = i > 0
            kw = jnp.concatenate([k_prev, ld(ko_ref, rows)], axis=0).astype(BF16)
            vw = jnp.concatenate([v_prev, ld(vo_ref, rows)], axis=0).astype(BF16)
            q4 = jnp.concatenate([q.astype(BF16)] * SW_HEADS, axis=0) * hm
            s = _dot(q4, kw, NT) + band_bias + jnp.where(has_prev, 0.0, prev_bias)
            return rows, s, vw

        def softmax(s):
            m = jnp.max(s, axis=-1, keepdims=True)
            pe = jnp.exp(s - m)
            return m, pe.astype(BF16), jnp.sum(pe, axis=-1, keepdims=True)

        def merge(rows, m, l, pv, pidx=pidx):
            acc = pv[0:B, :]
            mf = jnp.broadcast_to(m[0:B, :], (B, SW_WIDTH))
            lf = jnp.broadcast_to(l[0:B, :], (B, SW_WIDTH))
            for h in range(1, SW_HEADS):
                hs = slice(h * B, (h + 1) * B)
                sel = lane_head == h
                acc = jnp.where(sel, pv[hs, :], acc)
                mf = jnp.where(sel, m[hs, :], mf)
                lf = jnp.where(sel, l[hs, :], lf)
            if pidx > 0:
                m_old = ld(m_s, rows)
                m_new = jnp.maximum(m_old, mf)
                a_old = jnp.exp(m_old - m_new)
                a_cur = jnp.exp(mf - m_new)
                acc = ld(acc_s, rows) * a_old + acc * a_cur
                lf = ld(l_s, rows) * a_old + lf * a_cur
                mf = m_new
            return acc, mf, lf

        def group(gi, carry):
            units = [gi * AT_GROUP + t for t in range(AT_GROUP)]
            sc = [scores(u) for u in units]
            sm = [softmax(s) for _, s, _ in sc]
            pv = [_dot(pe, vw) for (_, pe, _), (_, _, vw) in zip(sm, sc)]
            out = [merge(rows, m, l, p) for (rows, _, _), (m, _, l), p in zip(sc, sm, pv)]
            for (rows, _, _), (acc, mf, lf) in zip(sc, out):
                st(acc_s, rows, acc)
                st(m_s, rows, mf)
                st(l_s, rows, lf)
            return carry

        lax.fori_loop(0, AT_SPAN // (B * AT_GROUP), group, 0)

    for j in range(nh):
        o_ref[:, j * LANES:(j + 1) * LANES] = acc_s[j][...] / l_s[j][...]


def _dilated_mix(z):
    for window, dil in SW_PATTERNS:
        assert window // dil == SW_BLOCK and AT_SPAN % (dil * SW_BLOCK) == 0
    nh = SW_WIDTH // LANES
    qcol = (HG_HEADS * 4 * HG_DIM) // LANES
    own = lambda c: pl.BlockSpec((AT_SPAN, LANES), lambda i: (i, qcol + c))
    prev = lambda c: pl.BlockSpec((AT_SPAN, LANES), lambda i: (jnp.maximum(i - 1, 0), qcol + c))
    row_head = np.arange(SW_HEADS * SW_BLOCK) // SW_BLOCK
    lane_head = np.arange(SW_WIDTH) // SW_HDIM
    hm = jnp.asarray(row_head[:, None] == lane_head[None, :], F32)
    scr = pltpu.VMEM((AT_SPAN, LANES), F32)
    return pl.pallas_call(
        _attn_kernel,
        grid=(SEQ // AT_SPAN,),
        in_specs=[own(c) for c in range(3 * nh)] + [prev(c) for c in range(nh, 3 * nh)] + [_vmem_spec()],
        out_specs=pl.BlockSpec((AT_SPAN, SW_WIDTH), lambda i: (i, 0)),
        out_shape=jax.ShapeDtypeStruct((SEQ, SW_WIDTH), F32),
        scratch_shapes=[scr] * (3 * nh),
        compiler_params=_params(("arbitrary",)),
        name="dilated_attn",
    )(*([z] * (5 * nh)), hm)


def _rwkv_masks():
    t = np.arange(RK_CHUNK)[:, None]
    s = np.tile(np.arange(RK_CHUNK), RK_HEADS)[None, :]
    masks = [s < t, s <= t, s == t]
    for lv in range(RK_LEVELS):
        masks.append(((t >> (lv + 1)) == (s >> (lv + 1))) & (((t >> lv) & 1) == 1) & (((s >> lv) & 1) == 0))
    key_eye = np.arange(RK_HDIM)[:, None] == np.tile(np.arange(RK_HDIM), RK_HEADS)[None, :]
    return np.stack(masks).astype(np.float32), key_eye.astype(np.float32)


def _head_mask(row_block, col_block):
    rh = np.arange(RK_HEADS * row_block) // row_block
    ch = np.arange(RK_HEADS * col_block) // col_block
    return rh[:, None] == ch[None, :]


M_STRICT, M_LOWER, M_EYE, M_LEVEL0 = 0, 1, 2, 3


def _rwkv_kernel(*refs, has_vres):
    (zr_ref, zk_ref, zv_ref, zl_ref, mu_ref, w0_ref, a0_ref, w2_ref, a2_ref, g2_ref,
     kkw_ref, kaw_ref, rkw_ref, lnw_ref, lnb_ref, bd_ref, tri_ref, msk_ref, keye_ref,
     hmn_ref, hmp_ref) = refs[:21]
    refs = refs[21:]
    if has_vres:
        vf_ref, v0_ref, v1_ref, v2_ref = refs[:4]
        refs = refs[4:]
        y_ref, st_s, cb_s = refs
    else:
        y_ref, vfo_ref, st_s, cb_s = refs
    C, T, HD = RK_CHUNK, RK_TILE, RK_HDIM

    @pl.when(pl.program_id(0) == 0)
    def _():
        st_s[...] = jnp.zeros_like(st_s)
        cb_s[...] = jnp.zeros_like(cb_s)

    def shift_mix(idx, ref):
        c = ref[...]
        cb_s[idx, 8:8 + T, :] = c
        cp = cb_s[idx, 7:7 + T, :]
        cb_s[idx, 0:8, :] = c[T - 8:T, :]
        return c + (cp - c) * mu_ref[idx:idx + 1, :]

    r = shift_mix(0, zr_ref)
    k0 = shift_mix(1, zk_ref)
    v = shift_mix(2, zv_ref)
    lo = shift_mix(3, zl_ref)

    bd = bd_ref[...]
    wpre = w0_ref[...] + _mm(jnp.tanh(lo), w2_ref[...])
    sp = jnp.maximum(-wpre, 0.0) + jnp.log(1.0 + jnp.exp(-jnp.abs(wpre)))
    logw = -jnp.exp(-sp - 0.5)
    a = _sigmoid(a0_ref[...] + _mm(lo, a2_ref[...]))
    g = _mm(_sigmoid(lo), g2_ref[...])
    if has_vres:
        mix = _sigmoid(v0_ref[...] + _mm(_mm(v, v1_ref[...]), v2_ref[...]))
        v = v + (vf_ref[...] - v) * mix
    else:
        vfo_ref[...] = v
    kk = k0 * kkw_ref[...]
    kk = kk * lax.rsqrt(jnp.maximum(_sel_right(kk * kk, bd, terms=2), RK_NORM_EPS * RK_NORM_EPS))
    k = k0 * (1.0 + (a - 1.0) * kaw_ref[...])
    beta = kk * a

    lw = jnp.concatenate([_sel_left(tri_ref[...], logw[i:i + RK_CUM, :], terms=2)
                          for i in range(0, T, RK_CUM)], axis=0)
    e_neg = jnp.exp(-lw)
    kap = kk * jnp.exp(lw - logw)
    bh = beta * e_neg
    kh = k * e_neg
    rh = r * jnp.exp(lw)
    hmn = hmn_ref[...]
    hmp = hmp_ref[...]
    hmn_f = hmn.astype(F32)

    def stack_n(x):
        return jnp.concatenate([x.astype(BF16)] * RK_HEADS, axis=0) * hmn

    def stack_p(x):
        return jnp.concatenate([x.astype(BF16)] * RK_HEADS, axis=0) * hmp

    def stack_s(x):
        return jnp.concatenate([x.astype(BF16)] * RK_HEADS, axis=0) * bd

    def packed_t(x):
        full = (jnp.concatenate([x] * RK_HEADS, axis=0) * hmn_f).T
        out = full[0:HD, :]
        for h in range(1, RK_HEADS):
            out = out + full[h * HD:(h + 1) * HD, :]
        return out

    def mm(x, w, dims=NN):
        return _dot(x.astype(BF16), w, dims)

    chunks = range(T // C)
    rows = [slice(c * C, (c + 1) * C) for c in chunks]
    each = lambda fn, *lists: [fn(*args) for args in zip(*lists)]
    lwc = [lw[(c + 1) * C - 1:(c + 1) * C, :] for c in chunks]
    e_end = [jnp.exp(lwc[c] - lw[rows[c], :]) for c in chunks]
    kap_c = [kap[rw, :] for rw in rows]
    rh_c = [rh[rw, :] for rw in rows]
    bh4 = [stack_n(bh[rw, :]) for rw in rows]
    kh4 = [stack_n(kh[rw, :]) for rw in rows]
    v4 = [stack_n(v[rw, :]) for rw in rows]
    rcat = lambda *xs: jnp.concatenate(xs, axis=0)
    kap_rh = each(rcat, kap_c, rh_c)
    with_b = each(lambda x, w: mm(x, w, NT), kap_rh, bh4)
    with_k = each(lambda x, w: mm(x, w, NT), kap_rh, kh4)
    g_kb = [t[0:C, :] for t in with_b]
    a_rb = [t[C:2 * C, :] * msk_ref[M_LOWER] for t in with_b]
    a_k = [t[0:C, :] * msk_ref[M_STRICT] for t in with_k]
    a_rk = [t[C:2 * C, :] * msk_ref[M_LOWER] for t in with_k]
    x = [msk_ref[M_EYE] - gc * msk_ref[M_LEVEL0] for gc in g_kb]
    for lv in range(1, RK_LEVELS):
        xl = each(lambda xc, gc: mm(xc, stack_p(gc * msk_ref[M_LEVEL0 + lv])), x, g_kb)
        x = each(lambda xc, xlc: xc - mm(xlc, stack_p(xc)), x, xl)
    m1 = each(lambda xc, kc: mm(xc, stack_n(kc)), x, kap_c)
    bwt = [packed_t(beta[rows[c], :] * e_end[c]) for c in chunks]
    kwt = [packed_t(k[rows[c], :] * e_end[c]) for c in chunks]
    with_v = each(lambda ak, ark, kw, vc: mm(rcat(ak, ark, kw), vc), a_k, a_rk, kwt, v4)
    m2 = each(lambda xc, t: mm(xc, stack_n(t[0:C, :])), x, with_v)
    arb_bwt = each(rcat, a_rb, bwt)
    with_m1 = each(lambda l, mc: mm(l, stack_n(mc)), arb_bwt, m1)
    with_m2 = each(lambda l, mc: mm(l, stack_n(mc)), arb_bwt, m2)
    q1 = each(lambda rc, t: rc - t[0:C, :], rh_c, with_m1)
    q2 = each(lambda tv, t: tv[C:2 * C, :] - t[0:C, :], with_v, with_m2)
    pt = each(lambda lc, t: keye_ref[...] * jnp.exp(lc) - t[C:C + HD, :], lwc, with_m1)
    rt = each(lambda tv, t: tv[2 * C:2 * C + HD, :] - t[C:C + HD, :], with_v, with_m2)

    st = st_s[...]
    ys = []
    for c in chunks:
        with_st = mm(rcat(q1[c], pt[c]), stack_s(st))
        ys.append(with_st[0:C, :] + q2[c])
        st = with_st[C:C + HD, :] + rt[c]
    st_s[...] = st
    y = jnp.concatenate(ys, axis=0)

    inv_n = 1.0 / RK_HDIM
    mean = _sel_right(y, bd, terms=2) * inv_n
    d = y - mean
    var = _sel_right(d * d, bd, terms=2) * inv_n
    yn = d * lax.rsqrt(var + RK_GN_EPS) * lnw_ref[...] + lnb_ref[...]
    bonus = _sel_right(r * k * rkw_ref[...], bd, terms=2) * v
    y_ref[...] = (yn + bonus) * g


def _rwkv_params(w, l):
    row = lambda a: a.reshape(1, -1)
    n_dec, n_a = w["rwkv_w2"].shape[1], w["rwkv_a2"].shape[1]
    prm = {
        "mu": w["rwkv_mu"][l].reshape(4, RK_WIDTH),
        "w0": row(w["rwkv_w0"][l]), "a0": row(w["rwkv_a0"][l]),
        "w2": _pad_rows(w["rwkv_w2"][l], 0, RK_WIDTH).astype(BF16),
        "a2": _pad_rows(w["rwkv_a2"][l], n_dec, RK_WIDTH).astype(BF16),
        "g2": _pad_rows(w["rwkv_g2"][l], n_dec + n_a, RK_WIDTH).astype(BF16),
        "k_k": row(w["rwkv_k_k"][l]), "k_a": row(w["rwkv_k_a"][l]), "r_k": row(w["rwkv_r_k"][l]),
        "ln_w": row(w["rwkv_ln_w"][l]), "ln_b": row(w["rwkv_ln_b"][l]),
    }
    if l > 0:
        v1, v2 = w["rwkv_v1"][l - 1], w["rwkv_v2"][l - 1]
        prm["v0"] = row(w["rwkv_v0"][l - 1])
        prm["v1"] = jnp.pad(v1, ((0, 0), (0, LANES - v1.shape[1]))).astype(BF16)
        prm["v2"] = _pad_rows(v2, 0, LANES).astype(BF16)
    return prm


def _rwkv(z, prm, v_first):
    has_vres = v_first is not None
    C, T = RK_CHUNK, RK_TILE
    assert (1 << RK_LEVELS) == C and T % C == 0 and RK_HEADS * RK_HDIM == RK_WIDTH
    ccol = (HG_HEADS * 4 * HG_DIM + 3 * SW_WIDTH) // RK_WIDTH
    zcol = lambda off: pl.BlockSpec((T, RK_WIDTH), lambda i: (i, ccol + off))
    row = pl.BlockSpec((T, RK_WIDTH), lambda i: (i, 0))
    lane = np.arange(RK_WIDTH)
    bd = jnp.asarray((lane[:, None] // RK_HDIM) == (lane[None, :] // RK_HDIM), BF16)
    t = np.arange(RK_CUM)
    tri = jnp.asarray(((t[:, None] // C) == (t[None, :] // C)) & (t[None, :] <= t[:, None]), BF16)
    packed_masks, key_eye = _rwkv_masks()
    args = [z, z, z, z, prm["mu"], prm["w0"], prm["a0"], prm["w2"], prm["a2"], prm["g2"],
            prm["k_k"], prm["k_a"], prm["r_k"], prm["ln_w"], prm["ln_b"], bd, tri,
            jnp.asarray(packed_masks), jnp.asarray(key_eye),
            jnp.asarray(_head_mask(C, RK_HDIM), BF16), jnp.asarray(_head_mask(C, C), BF16)]
    in_specs = [zcol(0), zcol(1), zcol(2), zcol(3)] + [_vmem_spec()] * 17
    if has_vres:
        args += [v_first, prm["v0"], prm["v1"], prm["v2"]]
        in_specs += [row] + [_vmem_spec()] * 3
        out_specs = row
        out_shape = jax.ShapeDtypeStruct((SEQ, RK_WIDTH), F32)
    else:
        out_specs = [row, row]
        out_shape = [jax.ShapeDtypeStruct((SEQ, RK_WIDTH), F32)] * 2
    out = pl.pallas_call(
        functools.partial(_rwkv_kernel, has_vres=has_vres),
        grid=(SEQ // T,),
        in_specs=in_specs,
        out_specs=out_specs,
        out_shape=out_shape,
        scratch_shapes=[pltpu.VMEM((RK_HDIM, RK_WIDTH), F32), pltpu.VMEM((4, T + 8, RK_WIDTH), F32)],
        compiler_params=_params(("arbitrary",)),
        name="rwkv7",
    )(*args)
    if has_vres:
        return out, v_first
    return out[0], out[1]


def _post_kernel(h_ref, oa_ref, ob_ref, oc_ref, p_ref, wout_ref, gffn_ref, wup_ref, cw_ref, cb_ref,
                 wdn_ref, gple_ref, wg_ref, wp_ref, gfin_ref, out_ref,
                 hn_s, carry_s, act_s, *, final):
    T = ROW_TILE

    @pl.when(pl.program_id(0) == 0)
    def _():
        carry_s[...] = jnp.zeros_like(carry_s)

    mix = (_dot(oa_ref[...].astype(BF16), wout_ref[0:HG_WIDTH, :])
           + _dot(ob_ref[...].astype(BF16), wout_ref[HG_WIDTH:HG_WIDTH + SW_WIDTH, :])
           + _dot(oc_ref[...].astype(BF16), wout_ref[HG_WIDTH + SW_WIDTH:, :]))
    h1 = h_ref[...] + mix
    hn_s[...] = _rms(h1, gffn_ref[...]).astype(BF16)

    rid = lax.broadcasted_iota(jnp.int32, (8, FFN_COLS), 0)

    def up(col):
        return _dot(hn_s[...], wup_ref[:, pl.ds(col, FFN_COLS)])

    def conv(col, u):
        cols = pl.ds(col, FFN_COLS)
        prev = carry_s[:, cols]
        carry_s[:, cols] = u[T - 8:T, :]
        r1 = pltpu.roll(u, 1, 0)
        r2 = pltpu.roll(u, 2, 0)
        top1 = jnp.where(rid == 0, prev[7:8, :], r1[0:8, :])
        top2 = jnp.where(rid == 0, prev[6:7, :], jnp.where(rid == 1, prev[7:8, :], r2[0:8, :]))
        u1 = jnp.concatenate([top1, r1[8:, :]], axis=0)
        u2 = jnp.concatenate([top2, r2[8:, :]], axis=0)
        return cb_ref[:, cols] + cw_ref[0:1, cols] * u2 + cw_ref[1:2, cols] * u1 + cw_ref[2:3, cols] * u

    def up_pair(j):
        return up(j * FFN_COLS), up(j * FFN_COLS + D_FF)

    n_steps = D_FF // FFN_COLS
    u_gate, u_val = up_pair(0)
    for j in range(n_steps):
        nxt = up_pair(j + 1) if j + 1 < n_steps else None
        gate = conv(j * FFN_COLS, u_gate)
        val = conv(j * FFN_COLS + D_FF, u_val)
        act_s[:, pl.ds(j * FFN_COLS, FFN_COLS)] = (_silu(gate) * val).astype(BF16)
        if nxt is not None:
            u_gate, u_val = nxt

    h2 = h1 + _dot(act_s[...], wdn_ref[...])
    gate = _sigmoid(_dot(_rms(h2, gple_ref[...]).astype(BF16), wg_ref[...]))
    pe = _dot(p_ref[...].astype(BF16), wp_ref[...])
    h3 = h2 + gate * pe
    out_ref[...] = _rms(h3, gfin_ref[...]) if final else h3


def _post(h, o_a, o_b, o_c, p_l, prm, final):
    T = ROW_TILE
    tile = lambda w: pl.BlockSpec((T, w), lambda i: (i, 0))
    return pl.pallas_call(
        functools.partial(_post_kernel, final=final),
        grid=(SEQ // T,),
        in_specs=[tile(D_MODEL), tile(HG_WIDTH), tile(SW_WIDTH), tile(RK_WIDTH), tile(PLE_DIM)]
                 + [_vmem_spec()] * 10,
        out_specs=tile(D_MODEL),
        out_shape=jax.ShapeDtypeStruct((SEQ, D_MODEL), F32),
        scratch_shapes=[pltpu.VMEM((T, D_MODEL), BF16),
                        pltpu.VMEM((8, 2 * D_FF), F32), pltpu.VMEM((T, D_FF), BF16)],
        compiler_params=_params(("arbitrary",)),
        name="outproj_ffn_ple",
    )(h, o_a, o_b, o_c, p_l, prm["w_out"], prm["g_ffn"], prm["w_up"], prm["conv_w"], prm["conv_b"],
      prm["w_down"], prm["g_ple"], prm["w_gate"], prm["w_ple"], prm["g_final"])


def _pad_rows(w, start, total):
    out = jnp.zeros((total, w.shape[1]), w.dtype)
    return lax.dynamic_update_slice(out, w, (start, 0))


def kernel(x, p, w_in, w_out, norm_mix_g, norm_ffn_g, norm_ple_g, final_norm_g, hgrn_lower_bounds, hgrn_gnorm_g, rwkv_mu, rwkv_w0, rwkv_w2, rwkv_a0, rwkv_a2, rwkv_g2, rwkv_k_k, rwkv_k_a, rwkv_r_k, rwkv_ln_w, rwkv_ln_b, rwkv_v0, rwkv_v1, rwkv_v2, ffn_up, ffn_conv_w, ffn_conv_b, ffn_down, ple_proj, ple_gate):
    row = lambda a: a.reshape(1, -1)
    h = x.reshape(SEQ, D_MODEL)
    v_first = None
    rwkv_w = dict(rwkv_mu=rwkv_mu, rwkv_w0=rwkv_w0, rwkv_w2=rwkv_w2, rwkv_a0=rwkv_a0, rwkv_a2=rwkv_a2,
                  rwkv_g2=rwkv_g2, rwkv_k_k=rwkv_k_k, rwkv_k_a=rwkv_k_a, rwkv_r_k=rwkv_r_k,
                  rwkv_ln_w=rwkv_ln_w, rwkv_ln_b=rwkv_ln_b, rwkv_v0=rwkv_v0, rwkv_v1=rwkv_v1, rwkv_v2=rwkv_v2)
    for l in range(DEPTH):
        z = _inproj(h, row(norm_mix_g[l]), w_in[l].astype(BF16))
        o_a = _hgrn(z, hgrn_lower_bounds, row(hgrn_gnorm_g[l]), l)
        o_b = _dilated_mix(z)
        o_c, v_first = _rwkv(z, _rwkv_params(rwkv_w, l), v_first)
        post = {
            "w_out": w_out[l].astype(BF16), "g_ffn": row(norm_ffn_g[l]),
            "w_up": ffn_up[l].astype(BF16), "conv_w": ffn_conv_w[l], "conv_b": row(ffn_conv_b[l]),
            "w_down": ffn_down[l].astype(BF16), "g_ple": row(norm_ple_g[l]),
            "w_gate": ple_gate[l].astype(BF16), "w_ple": ple_proj[l].astype(BF16),
            "g_final": row(final_norm_g),
        }
        h = _post(h, o_a, o_b, o_c, p[l].reshape(SEQ, PLE_DIM), post, final=l == DEPTH - 1)
    return h.reshape(1, SEQ, D_MODEL)
```

```python
import functools

import numpy as np
import jax
import jax.numpy as jnp
from jax import lax
from jax.experimental import pallas as pl
from jax.experimental.pallas import tpu as pltpu

F32 = jnp.float32
BF16 = jnp.bfloat16

D_MODEL = 1024
SEQ = 16384
DEPTH = 2
HG_HEADS = 4
HG_DIM = 128
HG_WIDTH = 512
SW_HEADS = 4
SW_HDIM = 64
SW_WIDTH = 256
SW_PATTERNS = ((128, 1), (512, 4), (2048, 16))
SW_BLOCK = 128
RK_HEADS = 4
RK_HDIM = 64
RK_WIDTH = 256
RK_GN_EPS = 64e-5
RK_NORM_EPS = 1e-12
N_IN = 3840
D_FF = 2816
PLE_DIM = 256
NORM_EPS = 1e-6

LANES = 128
SUBLANES = 8
MXU_DIM = 256
VMEM_LIMIT_BYTES = 56 * 1024 * 1024

IN_TILE = 1024
ROW_TILE = 512
FFN_COLS = 256
HG_TILE = 512
HG_CUM = MXU_DIM
HG_CHUNK = 64
HG_SAFE_EXP = 80.0
AT_SPAN = SW_BLOCK * max(d for _, d in SW_PATTERNS)
ATTN_MASKED = float("-inf")
AT_GROUP = 2
RK_CHUNK = 64
RK_TILE = 512
RK_LEVELS = 6
RK_CUM = MXU_DIM

NN = (((1,), (0,)), ((), ()))
NT = (((1,), (1,)), ((), ()))


def _dot(a, b, dims=NN):
    return lax.dot_general(a, b, dims, preferred_element_type=F32)


def _mm(a, b, dims=NN):
    return _dot(a.astype(BF16), b.astype(BF16), dims)


def _split3(x):
    x1 = x.astype(BF16)
    r1 = x - x1.astype(F32)
    x2 = r1.astype(BF16)
    x3 = (r1 - x2.astype(F32)).astype(BF16)
    return x1, x2, x3


def _sel_left(m01, x, terms=3):
    x1, x2, x3 = _split3(x)
    out = _dot(m01, x1) + _dot(m01, x2)
    return out + _dot(m01, x3) if terms == 3 else out


def _sel_right(x, m01, terms=3):
    x1, x2, x3 = _split3(x)
    out = _dot(x1, m01) + _dot(x2, m01)
    return out + _dot(x3, m01) if terms == 3 else out


def _sigmoid(x):
    return 1.0 / (1.0 + jnp.exp(-x))


def _silu(x):
    return x * _sigmoid(x)


def _rms(x, g):
    return x * lax.rsqrt(jnp.mean(x * x, axis=-1, keepdims=True) + NORM_EPS) * g


def _vmem_spec():
    return pl.BlockSpec(memory_space=pltpu.VMEM)


def _params(sem):
    return pltpu.CompilerParams(dimension_semantics=sem, vmem_limit_bytes=VMEM_LIMIT_BYTES)


def _inproj_kernel(h_ref, g_ref, w_ref, z_ref):
    y = _rms(h_ref[...], g_ref[...])
    z_ref[...] = _dot(y.astype(BF16), w_ref[...])


def _inproj(h, g, w_bf):
    return pl.pallas_call(
        _inproj_kernel,
        grid=(SEQ // IN_TILE,),
        in_specs=[pl.BlockSpec((IN_TILE, D_MODEL), lambda i: (i, 0)),
                  _vmem_spec(), _vmem_spec()],
        out_specs=pl.BlockSpec((IN_TILE, N_IN), lambda i: (i, 0)),
        out_shape=jax.ShapeDtypeStruct((SEQ, N_IN), F32),
        compiler_params=_params(("arbitrary",)),
        name="inproj",
    )(h, g, w_bf)


def _hgrn_kernel(q_ref, f_ref, i_ref, g_ref, lbp_ref, gn_ref, tri_ref, o_ref,
                 st_s, b_s, qq_s, kk_s, oi_s, raw_s, *, layer):
    C, T, D = HG_CHUNK, HG_TILE, HG_DIM

    @pl.when(pl.program_id(0) == 0)
    def _():
        st_s[...] = jnp.zeros_like(st_s)

    lbp = lbp_ref[...]
    e = jnp.exp(lbp - jnp.max(lbp, axis=0, keepdims=True))
    sm = e / jnp.sum(e, axis=0, keepdims=True)
    lb = jnp.zeros((1, HG_WIDTH), F32)
    for j in range(1, layer + 1):
        lb = lb + sm[j:j + 1, :]

    q = _silu(q_ref[...])
    f = lb + (1.0 - lb) * _sigmoid(f_ref[...])
    kk = 1.0 - f
    lf = jnp.log(f)
    b = jnp.concatenate([_sel_left(tri_ref[...], lf[i:i + HG_CUM, :], terms=2)
                         for i in range(0, T, HG_CUM)], axis=0)
    v = i_ref[...]
    b_s[...] = b
    qq_s[...] = q
    kk_s[...] = kk
    safe = jnp.min(b) > -HG_SAFE_EXP
    qe = (q * jnp.exp(b)).astype(BF16)
    kt = (kk * jnp.exp(-jnp.maximum(b, -HG_SAFE_EXP))).astype(BF16)
    vb = v.astype(BF16)
    half = lax.broadcasted_iota(jnp.int32, (2 * C, D), 0) < C
    causal = (lax.broadcasted_iota(jnp.int32, (C, C), 1) <= lax.broadcasted_iota(jnp.int32, (C, C), 0))

    heads = [slice(h * D, (h + 1) * D) for h in range(HG_HEADS)]
    units = [(n, h) for n in range(T // C) for h in range(HG_HEADS)]
    rows = [slice(n * C, (n + 1) * C) for n in range(T // C)]
    vts = [v[:, lanes].T.astype(BF16) for lanes in heads]
    tot = {(n, h): b[(n + 1) * C - 1:(n + 1) * C, heads[h]] for n, h in units}
    scores = {(n, h): _dot(qe[rows[n], heads[h]], kt[rows[n], heads[h]], NT) for n, h in units}
    intra = {(n, h): _dot(jnp.where(causal, scores[n, h], 0.0).astype(BF16), vb[rows[n], heads[h]])
             for n, h in units}
    upd = {}
    for n, h in units:
        slab = slice((n // 2) * 2 * C, (n // 2 + 1) * 2 * C)
        first = half if n % 2 == 0 else jnp.logical_not(half)
        ke = jnp.where(first, kk[slab, heads[h]] * jnp.exp(tot[n, h] - b[slab, heads[h]]), 0.0)
        upd[n, h] = _dot(vts[h][:, slab], ke.astype(BF16))
    state = {(0, h): st_s[h] for h in range(HG_HEADS)}
    for n, h in units:
        state[n + 1, h] = state[n, h] * jnp.exp(tot[n, h]) + upd[n, h]
    for h in range(HG_HEADS):
        st_s[h] = state[T // C, h]
    for n, h in units:
        inter = _dot(qe[rows[n], heads[h]], state[n, h].astype(BF16), NT)
        oi_s[rows[n], heads[h]] = inter
        raw_s[rows[n], heads[h]] = inter + intra[n, h]

    @pl.when(jnp.logical_not(safe))
    def _():
        rid = lax.broadcasted_iota(jnp.int32, (C, D), 0)

        def pick(x, i):
            return jnp.sum(jnp.where(rid == i, x, 0.0), axis=0, keepdims=True)

        for h in range(HG_HEADS):
            lanes = slice(h * D, (h + 1) * D)
            for n in range(T // C):
                rows = slice(n * C, (n + 1) * C)
                b_n, q_n, k_n, v_n = b_s[rows, lanes], qq_s[rows, lanes], kk_s[rows, lanes], i_ref[rows, lanes]

                def row(i, od, b_n=b_n, q_n=q_n, k_n=k_n, v_n=v_n):
                    w = jnp.exp(pick(b_n, i) - b_n) * (pick(q_n, i) * k_n)
                    s = jnp.sum(jnp.where(rid <= i, w, 0.0), axis=-1, keepdims=True)
                    return jnp.where(rid == i, jnp.sum(s * v_n, axis=0, keepdims=True), od)

                raw_s[rows, lanes] = oi_s[rows, lanes] + lax.fori_loop(0, C, row, jnp.zeros((C, D), F32))

    gate = gn_ref[...] * _silu(g_ref[...])
    for h in range(HG_HEADS):
        lanes = slice(h * D, (h + 1) * D)
        o = raw_s[:, lanes]
        o = o * lax.rsqrt(jnp.mean(o * o, axis=-1, keepdims=True) + NORM_EPS)
        o_ref[:, lanes] = (o * gate[:, lanes]).astype(o_ref.dtype)


def _hgrn(z, lb_params, gnorm_g, layer):
    T, C = HG_TILE, HG_CHUNK
    assert HG_DIM == LANES and T % (2 * C) == 0 and T % HG_CUM == 0 and HG_CUM % C == 0
    t = np.arange(HG_CUM)
    tri = jnp.asarray(((t[:, None] // C) == (t[None, :] // C)) & (t[None, :] <= t[:, None]), BF16)
    col = lambda j: pl.BlockSpec((T, HG_WIDTH), lambda i: (i, j))
    scr = pltpu.VMEM((T, HG_WIDTH), F32)
    return pl.pallas_call(
        functools.partial(_hgrn_kernel, layer=layer),
        grid=(SEQ // T,),
        in_specs=[col(0), col(1), col(2), col(3), _vmem_spec(), _vmem_spec(), _vmem_spec()],
        out_specs=pl.BlockSpec((T, HG_WIDTH), lambda i: (i, 0)),
        out_shape=jax.ShapeDtypeStruct((SEQ, HG_WIDTH), BF16),
        scratch_shapes=[pltpu.VMEM((HG_HEADS, HG_DIM, HG_DIM), F32), scr, scr, scr, scr, scr],
        compiler_params=_params(("arbitrary",)),
        name="hgrn2",
    )(z, z, z, z, lb_params, gnorm_g, tri)


def _attn_kernel(*refs):
    nh = SW_WIDTH // LANES
    q_ref, ko_ref, vo_ref, kp_ref, vp_ref = (refs[j * nh:(j + 1) * nh] for j in range(5))
    hm_ref, o_ref = refs[5 * nh], refs[5 * nh + 1]
    acc_s, m_s, l_s = (refs[5 * nh + 2 + j * nh:5 * nh + 2 + (j + 1) * nh] for j in range(3))

    def ld(halves, rows):
        return jnp.concatenate([h[rows, :] for h in halves], axis=1)

    def st(halves, rows, val):
        for j, h in enumerate(halves):
            h[rows, :] = val[:, j * LANES:(j + 1) * LANES]

    B = SW_BLOCK
    i = pl.program_id(0)
    hm = hm_ref[...].astype(BF16)
    shape = (SW_HEADS * B, 2 * B)
    qi = lax.broadcasted_iota(jnp.int32, shape, 0) & (B - 1)
    kj = lax.broadcasted_iota(jnp.int32, shape, 1)
    band_bias = jnp.where((kj >= qi) & (kj <= qi + B), 0.0, ATTN_MASKED)
    prev_bias = jnp.where(lax.broadcasted_iota(jnp.int32, (1, 2 * B), 1) < B, ATTN_MASKED, 0.0)
    lane_head = lax.broadcasted_iota(jnp.int32, (B, SW_WIDTH), 1) // SW_HDIM

    for pidx, (window, dil) in enumerate(SW_PATTERNS):
        per_res = AT_SPAN // (dil * B)

        def rows_at(s, dil=dil):
            if dil == 1:
                return pl.ds(pl.multiple_of(s, B), B)
            return pl.ds(s, B, stride=dil)

        def scores(u, dil=dil, per_res=per_res):
            r = u // per_res
            mb = u % per_res
            start = r + dil * B * mb
            rows = rows_at(start)
            q = ld(q_ref, rows) * (SW_HDIM ** -0.5)
            prow = pl.ds(AT_SPAN - dil * B + r, B, stride=dil) if dil > 1 else pl.ds(AT_SPAN - B, B)
            k_prev = ld(kp_ref, prow)
            v_prev = ld(vp_ref, prow)
            if per_res > 1:
                inner = mb > 0
                srows = rows_at(jnp.where(inner, start - dil * B, start))
                k_prev = jnp.where(inner, ld(ko_ref, srows), k_prev)
                v_prev = jnp.where(inner, ld(vo_ref, srows), v_prev)
                has_prev = inner | (i > 0)
            else:
                has_prev = i > 0
            kw = jnp.concatenate([k_prev, ld(ko_ref, rows)], axis=0).astype(BF16)
            vw = jnp.concatenate([v_prev, ld(vo_ref, rows)], axis=0).astype(BF16)
            q4 = jnp.concatenate([q.astype(BF16)] * SW_HEADS, axis=0) * hm
            s = _dot(q4, kw, NT) + band_bias + jnp.where(has_prev, 0.0, prev_bias)
            return rows, s, vw

        def softmax(s):
            m = jnp.max(s, axis=-1, keepdims=True)
            pe = jnp.exp(s - m)
            return m, pe.astype(BF16), jnp.sum(pe, axis=-1, keepdims=True)

        def merge(rows, m, l, pv, pidx=pidx):
            acc = pv[0:B, :]
            mf = jnp.broadcast_to(m[0:B, :], (B, SW_WIDTH))
            lf = jnp.broadcast_to(l[0:B, :], (B, SW_WIDTH))
            for h in range(1, SW_HEADS):
                hs = slice(h * B, (h + 1) * B)
                sel = lane_head == h
                acc = jnp.where(sel, pv[hs, :], acc)
                mf = jnp.where(sel, m[hs, :], mf)
                lf = jnp.where(sel, l[hs, :], lf)
            if pidx > 0:
                m_old = ld(m_s, rows)
                m_new = jnp.maximum(m_old, mf)
                a_old = jnp.exp(m_old - m_new)
                a_cur = jnp.exp(mf - m_new)
                acc = ld(acc_s, rows) * a_old + acc * a_cur
                lf = ld(l_s, rows) * a_old + lf * a_cur
                mf = m_new
            return acc, mf, lf

        def group(gi, carry):
            units = [gi * AT_GROUP + t for t in range(AT_GROUP)]
            sc = [scores(u) for u in units]
            sm = [softmax(s) for _, s, _ in sc]
            pv = [_dot(pe, vw) for (_, pe, _), (_, _, vw) in zip(sm, sc)]
            out = [merge(rows, m, l, p) for (rows, _, _), (m, _, l), p in zip(sc, sm, pv)]
            for (rows, _, _), (acc, mf, lf) in zip(sc, out):
                st(acc_s, rows, acc)
                st(m_s, rows, mf)
                st(l_s, rows, lf)
            return carry

        lax.fori_loop(0, AT_SPAN // (B * AT_GROUP), group, 0)

    for j in range(nh):
        o_ref[:, j * LANES:(j + 1) * LANES] = (acc_s[j][...] / l_s[j][...]).astype(o_ref.dtype)


def _dilated_mix(z):
    for window, dil in SW_PATTERNS:
        assert window // dil == SW_BLOCK and AT_SPAN % (dil * SW_BLOCK) == 0
    nh = SW_WIDTH // LANES
    qcol = (HG_HEADS * 4 * HG_DIM) // LANES
    own = lambda c: pl.BlockSpec((AT_SPAN, LANES), lambda i: (i, qcol + c))
    prev = lambda c: pl.BlockSpec((AT_SPAN, LANES), lambda i: (jnp.maximum(i - 1, 0), qcol + c))
    row_head = np.arange(SW_HEADS * SW_BLOCK) // SW_BLOCK
    lane_head = np.arange(SW_WIDTH) // SW_HDIM
    hm = jnp.asarray(row_head[:, None] == lane_head[None, :], F32)
    scr = pltpu.VMEM((AT_SPAN, LANES), F32)
    return pl.pallas_call(
        _attn_kernel,
        grid=(SEQ // AT_SPAN,),
        in_specs=[own(c) for c in range(3 * nh)] + [prev(c) for c in range(nh, 3 * nh)] + [_vmem_spec()],
        out_specs=pl.BlockSpec((AT_SPAN, SW_WIDTH), lambda i: (i, 0)),
        out_shape=jax.ShapeDtypeStruct((SEQ, SW_WIDTH), BF16),
        scratch_shapes=[scr] * (3 * nh),
        compiler_params=_params(("arbitrary",)),
        name="dilated_attn",
    )(*([z] * (5 * nh)), hm)


def _rwkv_masks():
    t = np.arange(RK_CHUNK)[:, None]
    s = np.tile(np.arange(RK_CHUNK), RK_HEADS)[None, :]
    masks = [s < t, s <= t, s == t]
    for lv in range(RK_LEVELS):
        masks.append(((t >> (lv + 1)) == (s >> (lv + 1))) & (((t >> lv) & 1) == 1) & (((s >> lv) & 1) == 0))
    key_eye = np.arange(RK_HDIM)[:, None] == np.tile(np.arange(RK_HDIM), RK_HEADS)[None, :]
    return np.stack(masks).astype(np.float32), key_eye.astype(np.float32)


def _head_mask(row_block, col_block):
    rh = np.arange(RK_HEADS * row_block) // row_block
    ch = np.arange(RK_HEADS * col_block) // col_block
    return rh[:, None] == ch[None, :]


M_STRICT, M_LOWER, M_EYE, M_LEVEL0 = 0, 1, 2, 3


def _rwkv_kernel(*refs, has_vres):
    (zr_ref, zk_ref, zv_ref, zl_ref, mu_ref, w0_ref, a0_ref, w2_ref, a2_ref, g2_ref,
     kkw_ref, kaw_ref, rkw_ref, lnw_ref, lnb_ref, bd_ref, tri_ref, msk_ref, keye_ref,
     hmn_ref, hmp_ref) = refs[:21]
    refs = refs[21:]
    if has_vres:
        vf_ref, v0_ref, v1_ref, v2_ref = refs[:4]
        refs = refs[4:]
        y_ref, st_s, cb_s = refs
    else:
        y_ref, vfo_ref, st_s, cb_s = refs
    C, T, HD = RK_CHUNK, RK_TILE, RK_HDIM

    @pl.when(pl.program_id(0) == 0)
    def _():
        st_s[...] = jnp.zeros_like(st_s)
        cb_s[...] = jnp.zeros_like(cb_s)

    def shift_mix(idx, ref):
        c = ref[...]
        cb_s[idx, SUBLANES:SUBLANES + T, :] = c
        cp = cb_s[idx, SUBLANES - 1:SUBLANES - 1 + T, :]
        cb_s[idx, 0:SUBLANES, :] = c[T - SUBLANES:T, :]
        return c + (cp - c) * mu_ref[idx:idx + 1, :]

    r = shift_mix(0, zr_ref)
    k0 = shift_mix(1, zk_ref)
    v = shift_mix(2, zv_ref)
    lo = shift_mix(3, zl_ref)

    bd = bd_ref[...]
    wpre = w0_ref[...] + _mm(jnp.tanh(lo), w2_ref[...])
    sp = jnp.maximum(-wpre, 0.0) + jnp.log(1.0 + jnp.exp(-jnp.abs(wpre)))
    logw = -jnp.exp(-sp - 0.5)
    a = _sigmoid(a0_ref[...] + _mm(lo, a2_ref[...]))
    g = _mm(_sigmoid(lo), g2_ref[...])
    if has_vres:
        mix = _sigmoid(v0_ref[...] + _mm(_mm(v, v1_ref[...]), v2_ref[...]))
        v = v + (vf_ref[...] - v) * mix
    else:
        vfo_ref[...] = v
    kk = k0 * kkw_ref[...]
    kk = kk * lax.rsqrt(jnp.maximum(_sel_right(kk * kk, bd, terms=2), RK_NORM_EPS * RK_NORM_EPS))
    k = k0 * (1.0 + (a - 1.0) * kaw_ref[...])
    beta = kk * a

    lw = jnp.concatenate([_sel_left(tri_ref[...], logw[i:i + RK_CUM, :], terms=2)
                          for i in range(0, T, RK_CUM)], axis=0)
    e_neg = jnp.exp(-lw)
    kap = kk * jnp.exp(lw - logw)
    bh = beta * e_neg
    kh = k * e_neg
    rh = r * jnp.exp(lw)
    hmn = hmn_ref[...]
    hmp = hmp_ref[...]
    hmn_f = hmn.astype(F32)

    def stack_n(x):
        return jnp.concatenate([x.astype(BF16)] * RK_HEADS, axis=0) * hmn

    def stack_p(x):
        return jnp.concatenate([x.astype(BF16)] * RK_HEADS, axis=0) * hmp

    def stack_s(x):
        return jnp.concatenate([x.astype(BF16)] * RK_HEADS, axis=0) * bd

    def packed_t(x):
        full = (jnp.concatenate([x] * RK_HEADS, axis=0) * hmn_f).T
        out = full[0:HD, :]
        for h in range(1, RK_HEADS):
            out = out + full[h * HD:(h + 1) * HD, :]
        return out

    def mm(x, w, dims=NN):
        return _dot(x.astype(BF16), w, dims)

    chunks = range(T // C)
    rows = [slice(c * C, (c + 1) * C) for c in chunks]
    each = lambda fn, *lists: [fn(*args) for args in zip(*lists)]
    lwc = [lw[(c + 1) * C - 1:(c + 1) * C, :] for c in chunks]
    e_end = [jnp.exp(lwc[c] - lw[rows[c], :]) for c in chunks]
    kap_c = [kap[rw, :] for rw in rows]
    rh_c = [rh[rw, :] for rw in rows]
    bh4 = [stack_n(bh[rw, :]) for rw in rows]
    kh4 = [stack_n(kh[rw, :]) for rw in rows]
    v4 = [stack_n(v[rw, :]) for rw in rows]
    rcat = lambda *xs: jnp.concatenate(xs, axis=0)
    kap_rh = each(rcat, kap_c, rh_c)
    with_b = each(lambda x, w: mm(x, w, NT), kap_rh, bh4)
    with_k = each(lambda x, w: mm(x, w, NT), kap_rh, kh4)
    g_kb = [t[0:C, :] for t in with_b]
    a_rb = [t[C:2 * C, :] * msk_ref[M_LOWER] for t in with_b]
    a_k = [t[0:C, :] * msk_ref[M_STRICT] for t in with_k]
    a_rk = [t[C:2 * C, :] * msk_ref[M_LOWER] for t in with_k]
    x = [msk_ref[M_EYE] - gc * msk_ref[M_LEVEL0] for gc in g_kb]
    for lv in range(1, RK_LEVELS):
        xl = each(lambda xc, gc: mm(xc, stack_p(gc * msk_ref[M_LEVEL0 + lv])), x, g_kb)
        x = each(lambda xc, xlc: xc - mm(xlc, stack_p(xc)), x, xl)
    m1 = each(lambda xc, kc: mm(xc, stack_n(kc)), x, kap_c)
    bwt = [packed_t(beta[rows[c], :] * e_end[c]) for c in chunks]
    kwt = [packed_t(k[rows[c], :] * e_end[c]) for c in chunks]
    with_v = each(lambda ak, ark, kw, vc: mm(rcat(ak, ark, kw), vc), a_k, a_rk, kwt, v4)
    m2 = each(lambda xc, t: mm(xc, stack_n(t[0:C, :])), x, with_v)
    arb_bwt = each(rcat, a_rb, bwt)
    with_m1 = each(lambda l, mc: mm(l, stack_n(mc)), arb_bwt, m1)
    with_m2 = each(lambda l, mc: mm(l, stack_n(mc)), arb_bwt, m2)
    q1 = each(lambda rc, t: rc - t[0:C, :], rh_c, with_m1)
    q2 = each(lambda tv, t: tv[C:2 * C, :] - t[0:C, :], with_v, with_m2)
    pt = each(lambda lc, t: keye_ref[...] * jnp.exp(lc) - t[C:C + HD, :], lwc, with_m1)
    rt = each(lambda tv, t: tv[2 * C:2 * C + HD, :] - t[C:C + HD, :], with_v, with_m2)

    st = st_s[...]
    ys = []
    for c in chunks:
        with_st = mm(rcat(q1[c], pt[c]), stack_s(st))
        ys.append(with_st[0:C, :] + q2[c])
        st = with_st[C:C + HD, :] + rt[c]
    st_s[...] = st
    y = jnp.concatenate(ys, axis=0)

    inv_n = 1.0 / RK_HDIM
    mean = _sel_right(y, bd, terms=2) * inv_n
    d = y - mean
    var = _sel_right(d * d, bd, terms=2) * inv_n
    yn = d * lax.rsqrt(var + RK_GN_EPS) * lnw_ref[...] + lnb_ref[...]
    bonus = _sel_right(r * k * rkw_ref[...], bd, terms=2) * v
    y_ref[...] = ((yn + bonus) * g).astype(y_ref.dtype)


def _rwkv_params(w, l):
    row = lambda a: a.reshape(1, -1)
    n_dec, n_a = w["rwkv_w2"].shape[1], w["rwkv_a2"].shape[1]
    prm = {
        "mu": w["rwkv_mu"][l].reshape(4, RK_WIDTH),
        "w0": row(w["rwkv_w0"][l]), "a0": row(w["rwkv_a0"][l]),
        "w2": _pad_rows(w["rwkv_w2"][l], 0, RK_WIDTH).astype(BF16),
        "a2": _pad_rows(w["rwkv_a2"][l], n_dec, RK_WIDTH).astype(BF16),
        "g2": _pad_rows(w["rwkv_g2"][l], n_dec + n_a, RK_WIDTH).astype(BF16),
        "k_k": row(w["rwkv_k_k"][l]), "k_a": row(w["rwkv_k_a"][l]), "r_k": row(w["rwkv_r_k"][l]),
        "ln_w": row(w["rwkv_ln_w"][l]), "ln_b": row(w["rwkv_ln_b"][l]),
    }
    if l > 0:
        v1, v2 = w["rwkv_v1"][l - 1], w["rwkv_v2"][l - 1]
        prm["v0"] = row(w["rwkv_v0"][l - 1])
        prm["v1"] = jnp.pad(v1, ((0, 0), (0, LANES - v1.shape[1]))).astype(BF16)
        prm["v2"] = _pad_rows(v2, 0, LANES).astype(BF16)
    return prm


def _rwkv(z, prm, v_first):
    has_vres = v_first is not None
    C, T = RK_CHUNK, RK_TILE
    assert (1 << RK_LEVELS) == C and T % C == 0 and RK_HEADS * RK_HDIM == RK_WIDTH
    ccol = (HG_HEADS * 4 * HG_DIM + 3 * SW_WIDTH) // RK_WIDTH
    zcol = lambda off: pl.BlockSpec((T, RK_WIDTH), lambda i: (i, ccol + off))
    row = pl.BlockSpec((T, RK_WIDTH), lambda i: (i, 0))
    lane = np.arange(RK_WIDTH)
    bd = jnp.asarray((lane[:, None] // RK_HDIM) == (lane[None, :] // RK_HDIM), BF16)
    t = np.arange(RK_CUM)
    tri = jnp.asarray(((t[:, None] // C) == (t[None, :] // C)) & (t[None, :] <= t[:, None]), BF16)
    packed_masks, key_eye = _rwkv_masks()
    args = [z, z, z, z, prm["mu"], prm["w0"], prm["a0"], prm["w2"], prm["a2"], prm["g2"],
            prm["k_k"], prm["k_a"], prm["r_k"], prm["ln_w"], prm["ln_b"], bd, tri,
            jnp.asarray(packed_masks), jnp.asarray(key_eye),
            jnp.asarray(_head_mask(C, RK_HDIM), BF16), jnp.asarray(_head_mask(C, C), BF16)]
    in_specs = [zcol(0), zcol(1), zcol(2), zcol(3)] + [_vmem_spec()] * 17
    if has_vres:
        args += [v_first, prm["v0"], prm["v1"], prm["v2"]]
        in_specs += [row] + [_vmem_spec()] * 3
        out_specs = row
        out_shape = jax.ShapeDtypeStruct((SEQ, RK_WIDTH), BF16)
    else:
        out_specs = [row, row]
        out_shape = [jax.ShapeDtypeStruct((SEQ, RK_WIDTH), BF16), jax.ShapeDtypeStruct((SEQ, RK_WIDTH), F32)]
    out = pl.pallas_call(
        functools.partial(_rwkv_kernel, has_vres=has_vres),
        grid=(SEQ // T,),
        in_specs=in_specs,
        out_specs=out_specs,
        out_shape=out_shape,
        scratch_shapes=[pltpu.VMEM((RK_HDIM, RK_WIDTH), F32),
                        pltpu.VMEM((4, T + SUBLANES, RK_WIDTH), F32)],
        compiler_params=_params(("arbitrary",)),
        name="rwkv7",
    )(*args)
    if has_vres:
        return out, v_first
    return out[0], out[1]


def _post_kernel(h_ref, oa_ref, ob_ref, oc_ref, p_ref, wout_ref, gffn_ref, wup_ref, cw_ref, cb_ref,
                 wdn_ref, gple_ref, wg_ref, wp_ref, gfin_ref, out_ref,
                 hn_s, carry_s, act_s, *, final):
    T = ROW_TILE

    @pl.when(pl.program_id(0) == 0)
    def _():
        carry_s[...] = jnp.zeros_like(carry_s)

    mix = (_dot(oa_ref[...].astype(BF16), wout_ref[0:HG_WIDTH, :])
           + _dot(ob_ref[...].astype(BF16), wout_ref[HG_WIDTH:HG_WIDTH + SW_WIDTH, :])
           + _dot(oc_ref[...].astype(BF16), wout_ref[HG_WIDTH + SW_WIDTH:, :]))
    h1 = h_ref[...] + mix
    hn_s[...] = _rms(h1, gffn_ref[...]).astype(BF16)

    S8 = SUBLANES
    rid = lax.broadcasted_iota(jnp.int32, (S8, FFN_COLS), 0)

    def up(col):
        return _dot(hn_s[...], wup_ref[:, pl.ds(col, FFN_COLS)])

    def conv(col, u):
        cols = pl.ds(col, FFN_COLS)
        prev = carry_s[:, cols]
        carry_s[:, cols] = u[T - S8:T, :]
        r1 = pltpu.roll(u, 1, 0)
        r2 = pltpu.roll(u, 2, 0)
        last, last2 = prev[S8 - 1:S8, :], prev[S8 - 2:S8 - 1, :]
        top1 = jnp.where(rid == 0, last, r1[0:S8, :])
        top2 = jnp.where(rid == 0, last2, jnp.where(rid == 1, last, r2[0:S8, :]))
        u1 = jnp.concatenate([top1, r1[S8:, :]], axis=0)
        u2 = jnp.concatenate([top2, r2[S8:, :]], axis=0)
        return cb_ref[:, cols] + cw_ref[0:1, cols] * u2 + cw_ref[1:2, cols] * u1 + cw_ref[2:3, cols] * u

    def up_pair(j):
        return up(j * FFN_COLS), up(j * FFN_COLS + D_FF)

    n_steps = D_FF // FFN_COLS
    u_gate, u_val = up_pair(0)
    for j in range(n_steps):
        nxt = up_pair(j + 1) if j + 1 < n_steps else None
        gate = conv(j * FFN_COLS, u_gate)
        val = conv(j * FFN_COLS + D_FF, u_val)
        act_s[:, pl.ds(j * FFN_COLS, FFN_COLS)] = (_silu(gate) * val).astype(BF16)
        if nxt is not None:
            u_gate, u_val = nxt

    h2 = h1 + _dot(act_s[...], wdn_ref[...])
    gate = _sigmoid(_dot(_rms(h2, gple_ref[...]).astype(BF16), wg_ref[...]))
    pe = _dot(p_ref[...].astype(BF16), wp_ref[...])
    h3 = h2 + gate * pe
    out_ref[...] = _rms(h3, gfin_ref[...]) if final else h3


def _post(h, o_a, o_b, o_c, p_l, prm, final):
    T = ROW_TILE
    tile = lambda w: pl.BlockSpec((T, w), lambda i: (i, 0))
    return pl.pallas_call(
        functools.partial(_post_kernel, final=final),
        grid=(SEQ // T,),
        in_specs=[tile(D_MODEL), tile(HG_WIDTH), tile(SW_WIDTH), tile(RK_WIDTH), tile(PLE_DIM)]
                 + [_vmem_spec()] * 10,
        out_specs=tile(D_MODEL),
        out_shape=jax.ShapeDtypeStruct((SEQ, D_MODEL), F32),
        scratch_shapes=[pltpu.VMEM((T, D_MODEL), BF16),
                        pltpu.VMEM((SUBLANES, 2 * D_FF), F32), pltpu.VMEM((T, D_FF), BF16)],
        compiler_params=_params(("arbitrary",)),
        name="outproj_ffn_ple",
    )(h, o_a, o_b, o_c, p_l, prm["w_out"], prm["g_ffn"], prm["w_up"], prm["conv_w"], prm["conv_b"],
      prm["w_down"], prm["g_ple"], prm["w_gate"], prm["w_ple"], prm["g_final"])


def _pad_rows(w, start, total):
    out = jnp.zeros((total, w.shape[1]), w.dtype)
    return lax.dynamic_update_slice(out, w, (start, 0))


def kernel(x, p, w_in, w_out, norm_mix_g, norm_ffn_g, norm_ple_g, final_norm_g, hgrn_lower_bounds, hgrn_gnorm_g, rwkv_mu, rwkv_w0, rwkv_w2, rwkv_a0, rwkv_a2, rwkv_g2, rwkv_k_k, rwkv_k_a, rwkv_r_k, rwkv_ln_w, rwkv_ln_b, rwkv_v0, rwkv_v1, rwkv_v2, ffn_up, ffn_conv_w, ffn_conv_b, ffn_down, ple_proj, ple_gate):
    row = lambda a: a.reshape(1, -1)
    h = x.reshape(SEQ, D_MODEL)
    v_first = None
    rwkv_w = dict(rwkv_mu=rwkv_mu, rwkv_w0=rwkv_w0, rwkv_w2=rwkv_w2, rwkv_a0=rwkv_a0, rwkv_a2=rwkv_a2,
                  rwkv_g2=rwkv_g2, rwkv_k_k=rwkv_k_k, rwkv_k_a=rwkv_k_a, rwkv_r_k=rwkv_r_k,
                  rwkv_ln_w=rwkv_ln_w, rwkv_ln_b=rwkv_ln_b, rwkv_v0=rwkv_v0, rwkv_v1=rwkv_v1, rwkv_v2=rwkv_v2)
    for l in range(DEPTH):
        z = _inproj(h, row(norm_mix_g[l]), w_in[l].astype(BF16))
        o_a = _hgrn(z, hgrn_lower_bounds, row(hgrn_gnorm_g[l]), l)
        o_b = _dilated_mix(z)
        o_c, v_first = _rwkv(z, _rwkv_params(rwkv_w, l), v_first)
        post = {
            "w_out": w_out[l].astype(BF16), "g_ffn": row(norm_ffn_g[l]),
            "w_up": ffn_up[l].astype(BF16), "conv_w": ffn_conv_w[l], "conv_b": row(ffn_conv_b[l]),
            "w_down": ffn_down[l].astype(BF16), "g_ple": row(norm_ple_g[l]),
            "w_gate": ple_gate[l].astype(BF16), "w_ple": ple_proj[l].astype(BF16),
            "g_final": row(final_norm_g),
        }
        h = _post(h, o_a, o_b, o_c, p[l].reshape(SEQ, PLE_DIM), post, final=l == DEPTH - 1)
    return h.reshape(1, SEQ, D_MODEL)
```

```python
import functools

import numpy as np
import jax
import jax.numpy as jnp
from jax import lax
from jax.experimental import pallas as pl
from jax.experimental.pallas import tpu as pltpu

F32 = jnp.float32
BF16 = jnp.bfloat16

D_MODEL = 1024
SEQ = 16384
DEPTH = 2
HG_HEADS = 4
HG_DIM = 128
HG_WIDTH = 512
SW_HEADS = 4
SW_HDIM = 64
SW_WIDTH = 256
SW_PATTERNS = ((128, 1), (512, 4), (2048, 16))
SW_BLOCK = 128
RK_HEADS = 4
RK_HDIM = 64
RK_WIDTH = 256
RK_GN_EPS = 64e-5
RK_NORM_EPS = 1e-12
N_IN = 3840
D_FF = 2816
PLE_DIM = 256
NORM_EPS = 1e-6

LANES = 128
SUBLANES = 8
MXU_DIM = 256
VMEM_LIMIT_BYTES = 56 * 1024 * 1024

IN_TILE = 1024
ROW_TILE = 512
FFN_COLS = 256
HG_TILE = 512
HG_CUM = MXU_DIM
HG_CHUNK = 64
HG_SAFE_EXP = 80.0
AT_SPAN = SW_BLOCK * max(d for _, d in SW_PATTERNS)
ATTN_MASKED = float("-inf")
AT_GROUP = 2
RK_CHUNK = 64
RK_TILE = 512
RK_LEVELS = 6
RK_CUM = MXU_DIM

NN = (((1,), (0,)), ((), ()))
NT = (((1,), (1,)), ((), ()))


def _dot(a, b, dims=NN):
    return lax.dot_general(a, b, dims, preferred_element_type=F32)


def _mm(a, b, dims=NN):
    return _dot(a.astype(BF16), b.astype(BF16), dims)


def _split3(x):
    x1 = x.astype(BF16)
    r1 = x - x1.astype(F32)
    x2 = r1.astype(BF16)
    x3 = (r1 - x2.astype(F32)).astype(BF16)
    return x1, x2, x3


def _sel_left(m01, x, terms=3):
    x1, x2, x3 = _split3(x)
    out = _dot(m01, x1) + _dot(m01, x2)
    return out + _dot(m01, x3) if terms == 3 else out


def _sel_right(x, m01, terms=3):
    x1, x2, x3 = _split3(x)
    out = _dot(x1, m01) + _dot(x2, m01)
    return out + _dot(x3, m01) if terms == 3 else out


def _sigmoid(x):
    return 1.0 / (1.0 + jnp.exp(-x))


def _silu(x):
    return x * _sigmoid(x)


def _rms(x, g):
    return x * lax.rsqrt(jnp.mean(x * x, axis=-1, keepdims=True) + NORM_EPS) * g


def _vmem_spec():
    return pl.BlockSpec(memory_space=pltpu.VMEM)


def _params(sem):
    return pltpu.CompilerParams(dimension_semantics=sem, vmem_limit_bytes=VMEM_LIMIT_BYTES)


def _inproj_kernel(h_ref, g_ref, w_ref, z_ref):
    y = _rms(h_ref[...], g_ref[...])
    z_ref[...] = _dot(y.astype(BF16), w_ref[...])


def _inproj(h, g, w_bf):
    return pl.pallas_call(
        _inproj_kernel,
        grid=(SEQ // IN_TILE,),
        in_specs=[pl.BlockSpec((IN_TILE, D_MODEL), lambda i: (i, 0)),
                  _vmem_spec(), _vmem_spec()],
        out_specs=pl.BlockSpec((IN_TILE, N_IN), lambda i: (i, 0)),
        out_shape=jax.ShapeDtypeStruct((SEQ, N_IN), F32),
        compiler_params=_params(("arbitrary",)),
        name="inproj",
    )(h, g, w_bf)


def _hgrn_kernel(q_ref, f_ref, i_ref, g_ref, lbp_ref, gn_ref, tri_ref, o_ref,
                 st_s, b_s, qq_s, kk_s, oi_s, raw_s, *, layer):
    C, T, D = HG_CHUNK, HG_TILE, HG_DIM

    @pl.when(pl.program_id(0) == 0)
    def _():
        st_s[...] = jnp.zeros_like(st_s)

    lbp = lbp_ref[...]
    e = jnp.exp(lbp - jnp.max(lbp, axis=0, keepdims=True))
    sm = e / jnp.sum(e, axis=0, keepdims=True)
    lb = jnp.zeros((1, HG_WIDTH), F32)
    for j in range(1, layer + 1):
        lb = lb + sm[j:j + 1, :]

    q = _silu(q_ref[...])
    f = lb + (1.0 - lb) * _sigmoid(f_ref[...])
    kk = 1.0 - f
    lf = jnp.log(f)
    b = jnp.concatenate([_sel_left(tri_ref[...], lf[i:i + HG_CUM, :], terms=2)
                         for i in range(0, T, HG_CUM)], axis=0)
    v = i_ref[...]
    b_s[...] = b
    qq_s[...] = q
    kk_s[...] = kk
    safe = jnp.min(b) > -HG_SAFE_EXP
    qe = (q * jnp.exp(b)).astype(BF16)
    kt = (kk * jnp.exp(-jnp.maximum(b, -HG_SAFE_EXP))).astype(BF16)
    vb = v.astype(BF16)
    half = lax.broadcasted_iota(jnp.int32, (2 * C, D), 0) < C
    causal = (lax.broadcasted_iota(jnp.int32, (C, C), 1) <= lax.broadcasted_iota(jnp.int32, (C, C), 0))

    heads = [slice(h * D, (h + 1) * D) for h in range(HG_HEADS)]
    units = [(n, h) for n in range(T // C) for h in range(HG_HEADS)]
    rows = [slice(n * C, (n + 1) * C) for n in range(T // C)]
    vts = [v[:, lanes].T.astype(BF16) for lanes in heads]
    tot = {(n, h): b[(n + 1) * C - 1:(n + 1) * C, heads[h]] for n, h in units}
    scores = {(n, h): _dot(qe[rows[n], heads[h]], kt[rows[n], heads[h]], NT) for n, h in units}
    intra = {(n, h): _dot(jnp.where(causal, scores[n, h], 0.0).astype(BF16), vb[rows[n], heads[h]])
             for n, h in units}
    upd = {}
    for n, h in units:
        slab = slice((n // 2) * 2 * C, (n // 2 + 1) * 2 * C)
        first = half if n % 2 == 0 else jnp.logical_not(half)
        ke = jnp.where(first, kk[slab, heads[h]] * jnp.exp(tot[n, h] - b[slab, heads[h]]), 0.0)
        upd[n, h] = _dot(vts[h][:, slab], ke.astype(BF16))
    state = {(0, h): st_s[h] for h in range(HG_HEADS)}
    for n, h in units:
        state[n + 1, h] = state[n, h] * jnp.exp(tot[n, h]) + upd[n, h]
    for h in range(HG_HEADS):
        st_s[h] = state[T // C, h]
    for n, h in units:
        inter = _dot(qe[rows[n], heads[h]], state[n, h].astype(BF16), NT)
        oi_s[rows[n], heads[h]] = inter
        raw_s[rows[n], heads[h]] = inter + intra[n, h]

    @pl.when(jnp.logical_not(safe))
    def _():
        rid = lax.broadcasted_iota(jnp.int32, (C, D), 0)

        def pick(x, i):
            return jnp.sum(jnp.where(rid == i, x, 0.0), axis=0, keepdims=True)

        for h in range(HG_HEADS):
            lanes = slice(h * D, (h + 1) * D)
            for n in range(T // C):
                rows = slice(n * C, (n + 1) * C)
                b_n, q_n, k_n, v_n = b_s[rows, lanes], qq_s[rows, lanes], kk_s[rows, lanes], i_ref[rows, lanes]

                def row(i, od, b_n=b_n, q_n=q_n, k_n=k_n, v_n=v_n):
                    w = jnp.exp(pick(b_n, i) - b_n) * (pick(q_n, i) * k_n)
                    s = jnp.sum(jnp.where(rid <= i, w, 0.0), axis=-1, keepdims=True)
                    return jnp.where(rid == i, jnp.sum(s * v_n, axis=0, keepdims=True), od)

                raw_s[rows, lanes] = oi_s[rows, lanes] + lax.fori_loop(0, C, row, jnp.zeros((C, D), F32))

    gate = gn_ref[...] * _silu(g_ref[...])
    for h in range(HG_HEADS):
        lanes = slice(h * D, (h + 1) * D)
        o = raw_s[:, lanes]
        o = o * lax.rsqrt(jnp.mean(o * o, axis=-1, keepdims=True) + NORM_EPS)
        o_ref[:, lanes] = (o * gate[:, lanes]).astype(o_ref.dtype)


def _hgrn(z, lb_params, gnorm_g, layer):
    T, C = HG_TILE, HG_CHUNK
    assert HG_DIM == LANES and T % (2 * C) == 0 and T % HG_CUM == 0 and HG_CUM % C == 0
    t = np.arange(HG_CUM)
    tri = jnp.asarray(((t[:, None] // C) == (t[None, :] // C)) & (t[None, :] <= t[:, None]), BF16)
    col = lambda j: pl.BlockSpec((T, HG_WIDTH), lambda i: (i, j))
    scr = pltpu.VMEM((T, HG_WIDTH), F32)
    return pl.pallas_call(
        functools.partial(_hgrn_kernel, layer=layer),
        grid=(SEQ // T,),
        in_specs=[col(0), col(1), col(2), col(3), _vmem_spec(), _vmem_spec(), _vmem_spec()],
        out_specs=pl.BlockSpec((T, HG_WIDTH), lambda i: (i, 0)),
        out_shape=jax.ShapeDtypeStruct((SEQ, HG_WIDTH), BF16),
        scratch_shapes=[pltpu.VMEM((HG_HEADS, HG_DIM, HG_DIM), F32), scr, scr, scr, scr, scr],
        compiler_params=_params(("arbitrary",)),
        name="hgrn2",
    )(z, z, z, z, lb_params, gnorm_g, tri)


def _attn_kernel(*refs):
    nh = SW_WIDTH // LANES
    q_ref, ko_ref, vo_ref, kp_ref, vp_ref = (refs[j * nh:(j + 1) * nh] for j in range(5))
    hm_ref, o_ref = refs[5 * nh], refs[5 * nh + 1]
    acc_s, m_s, l_s = (refs[5 * nh + 2 + j * nh:5 * nh + 2 + (j + 1) * nh] for j in range(3))

    def ld(halves, rows):
        return jnp.concatenate([h[rows, :] for h in halves], axis=1)

    def st(halves, rows, val):
        for j, h in enumerate(halves):
            h[rows, :] = val[:, j * LANES:(j + 1) * LANES]

    B = SW_BLOCK
    i = pl.program_id(0)
    hm = hm_ref[...].astype(BF16)
    shape = (SW_HEADS * B, 2 * B)
    qi = lax.broadcasted_iota(jnp.int32, shape, 0) & (B - 1)
    kj = lax.broadcasted_iota(jnp.int32, shape, 1)
    band_bias = jnp.where((kj >= qi) & (kj <= qi + B), 0.0, ATTN_MASKED)
    prev_bias = jnp.where(lax.broadcasted_iota(jnp.int32, (1, 2 * B), 1) < B, ATTN_MASKED, 0.0)
    lane_head = lax.broadcasted_iota(jnp.int32, (B, SW_WIDTH), 1) // SW_HDIM

    for pidx, (window, dil) in enumerate(SW_PATTERNS):
        per_res = AT_SPAN // (dil * B)

        def rows_at(s, dil=dil):
            if dil == 1:
                return pl.ds(pl.multiple_of(s, B), B)
            return pl.ds(s, B, stride=dil)

        def scores(u, dil=dil, per_res=per_res):
            r = u // per_res
            mb = u % per_res
            start = r + dil * B * mb
            rows = rows_at(start)
            q = ld(q_ref, rows) * (SW_HDIM ** -0.5)
            prow = pl.ds(AT_SPAN - dil * B + r, B, stride=dil) if dil > 1 else pl.ds(AT_SPAN - B, B)
            k_prev = ld(kp_ref, prow)
            v_prev = ld(vp_ref, prow)
            if per_res > 1:
                inner = mb > 0
                srows = rows_at(jnp.where(inner, start - dil * B, start))
                k_prev = jnp.where(inner, ld(ko_ref, srows), k_prev)
                v_prev = jnp.where(inner, ld(vo_ref, srows), v_prev)
                has_prev = inner | (i > 0)
            else:
                has_prev = i > 0
            kw = jnp.concatenate([k_prev, ld(ko_ref, rows)], axis=0).astype(BF16)
            vw = jnp.concatenate([v_prev, ld(vo_ref, rows)], axis=0).astype(BF16)
            q4 = jnp.concatenate([q.astype(BF16)] * SW_HEADS, axis=0) * hm
            s = _dot(q4, kw, NT) + band_bias + jnp.where(has_prev, 0.0, prev_bias)
            return rows, s, vw

        def softmax(s):
            m = jnp.max(s, axis=-1, keepdims=True)
            pe = jnp.exp(s - m)
            return m, pe.astype(BF16), jnp.sum(pe, axis=-1, keepdims=True)

        def merge(rows, m, l, pv, pidx=pidx):
            acc = pv[0:B, :]
            mf = jnp.broadcast_to(m[0:B, :], (B, SW_WIDTH))
            lf = jnp.broadcast_to(l[0:B, :], (B, SW_WIDTH))
            for h in range(1, SW_HEADS):
                hs = slice(h * B, (h + 1) * B)
                sel = lane_head == h
                acc = jnp.where(sel, pv[hs, :], acc)
                mf = jnp.where(sel, m[hs, :], mf)
                lf = jnp.where(sel, l[hs, :], lf)
            if pidx > 0:
                m_old = ld(m_s, rows)
                m_new = jnp.maximum(m_old, mf)
                a_old = jnp.exp(m_old - m_new)
                a_cur = jnp.exp(mf - m_new)
                acc = ld(acc_s, rows) * a_old + acc * a_cur
                lf = ld(l_s, rows) * a_old + lf * a_cur
                mf = m_new
            return acc, mf, lf

        def group(gi, carry):
            units = [gi * AT_GROUP + t for t in range(AT_GROUP)]
            sc = [scores(u) for u in units]
            sm = [softmax(s) for _, s, _ in sc]
            pv = [_dot(pe, vw) for (_, pe, _), (_, _, vw) in zip(sm, sc)]
            out = [merge(rows, m, l, p) for (rows, _, _), (m, _, l), p in zip(sc, sm, pv)]
            for (rows, _, _), (acc, mf, lf) in zip(sc, out):
                st(acc_s, rows, acc)
                st(m_s, rows, mf)
                st(l_s, rows, lf)
            return carry

        lax.fori_loop(0, AT_SPAN // (B * AT_GROUP), group, 0)

    for j in range(nh):
        o_ref[:, j * LANES:(j + 1) * LANES] = (acc_s[j][...] / l_s[j][...]).astype(o_ref.dtype)


def _dilated_mix(z):
    for window, dil in SW_PATTERNS:
        assert window // dil == SW_BLOCK and AT_SPAN % (dil * SW_BLOCK) == 0
    nh = SW_WIDTH // LANES
    qcol = (HG_HEADS * 4 * HG_DIM) // LANES
    own = lambda c: pl.BlockSpec((AT_SPAN, LANES), lambda i: (i, qcol + c))
    prev = lambda c: pl.BlockSpec((AT_SPAN, LANES), lambda i: (jnp.maximum(i - 1, 0), qcol + c))
    row_head = np.arange(SW_HEADS * SW_BLOCK) // SW_BLOCK
    lane_head = np.arange(SW_WIDTH) // SW_HDIM
    hm = jnp.asarray(row_head[:, None] == lane_head[None, :], F32)
    scr = pltpu.VMEM((AT_SPAN, LANES), F32)
    return pl.pallas_call(
        _attn_kernel,
        grid=(SEQ // AT_SPAN,),
        in_specs=[own(c) for c in range(3 * nh)] + [prev(c) for c in range(nh, 3 * nh)] + [_vmem_spec()],
        out_specs=pl.BlockSpec((AT_SPAN, SW_WIDTH), lambda i: (i, 0)),
        out_shape=jax.ShapeDtypeStruct((SEQ, SW_WIDTH), BF16),
        scratch_shapes=[scr] * (3 * nh),
        compiler_params=_params(("arbitrary",)),
        name="dilated_attn",
    )(*([z] * (5 * nh)), hm)


def _rwkv_masks():
    t = np.arange(RK_CHUNK)[:, None]
    s = np.tile(np.arange(RK_CHUNK), RK_HEADS)[None, :]
    masks = [s < t, s <= t, s == t]
    for lv in range(RK_LEVELS):
        masks.append(((t >> (lv + 1)) == (s >> (lv + 1))) & (((t >> lv) & 1) == 1) & (((s >> lv) & 1) == 0))
    key_eye = np.arange(RK_HDIM)[:, None] == np.tile(np.arange(RK_HDIM), RK_HEADS)[None, :]
    return np.stack(masks).astype(np.float32), key_eye.astype(np.float32)


def _head_mask(row_block, col_block):
    rh = np.arange(RK_HEADS * row_block) // row_block
    ch = np.arange(RK_HEADS * col_block) // col_block
    return rh[:, None] == ch[None, :]


M_STRICT, M_LOWER, M_EYE, M_LEVEL0 = 0, 1, 2, 3


def _rwkv_kernel(*refs, has_vres):
    (zr_ref, zk_ref, zv_ref, zl_ref, mu_ref, w0_ref, a0_ref, w2_ref, a2_ref, g2_ref,
     kkw_ref, kaw_ref, rkw_ref, lnw_ref, lnb_ref, bd_ref, tri_ref, msk_ref, keye_ref,
     hmn_ref, hmp_ref) = refs[:21]
    refs = refs[21:]
    if has_vres:
        vf_ref, v0_ref, v1_ref, v2_ref = refs[:4]
        refs = refs[4:]
        y_ref, st_s, cb_s = refs
    else:
        y_ref, vfo_ref, st_s, cb_s = refs
    C, T, HD = RK_CHUNK, RK_TILE, RK_HDIM

    @pl.when(pl.program_id(0) == 0)
    def _():
        st_s[...] = jnp.zeros_like(st_s)
        cb_s[...] = jnp.zeros_like(cb_s)

    def shift_mix(idx, ref):
        c = ref[...]
        last = cb_s[idx, SUBLANES - 1:SUBLANES, :]
        cb_s[idx, 0:SUBLANES, :] = c[T - SUBLANES:T, :]
        rolled = pltpu.roll(c, 1, 0)
        rid = lax.broadcasted_iota(jnp.int32, (SUBLANES, RK_WIDTH), 0)
        cp = jnp.concatenate([jnp.where(rid == 0, last, rolled[0:SUBLANES, :]), rolled[SUBLANES:, :]], axis=0)
        return c + (cp - c) * mu_ref[idx:idx + 1, :]

    r = shift_mix(0, zr_ref)
    k0 = shift_mix(1, zk_ref)
    v = shift_mix(2, zv_ref)
    lo = shift_mix(3, zl_ref)

    bd = bd_ref[...]
    wpre = w0_ref[...] + _mm(jnp.tanh(lo), w2_ref[...])
    sp = jnp.maximum(-wpre, 0.0) + jnp.log(1.0 + jnp.exp(-jnp.abs(wpre)))
    logw = -jnp.exp(-sp - 0.5)
    a = _sigmoid(a0_ref[...] + _mm(lo, a2_ref[...]))
    g = _mm(_sigmoid(lo), g2_ref[...])
    if has_vres:
        mix = _sigmoid(v0_ref[...] + _mm(_mm(v, v1_ref[...]), v2_ref[...]))
        v = v + (vf_ref[...] - v) * mix
    else:
        vfo_ref[...] = v
    kk = k0 * kkw_ref[...]
    kk = kk * lax.rsqrt(jnp.maximum(_sel_right(kk * kk, bd, terms=2), RK_NORM_EPS * RK_NORM_EPS))
    k = k0 * (1.0 + (a - 1.0) * kaw_ref[...])
    beta = kk * a

    lw = jnp.concatenate([_sel_left(tri_ref[...], logw[i:i + RK_CUM, :], terms=2)
                          for i in range(0, T, RK_CUM)], axis=0)
    e_neg = jnp.exp(-lw)
    kap = kk * jnp.exp(lw - logw)
    bh = beta * e_neg
    kh = k * e_neg
    rh = r * jnp.exp(lw)
    hmn = hmn_ref[...]
    hmp = hmp_ref[...]
    hmn_f = hmn.astype(F32)

    def stack_n(x):
        return jnp.concatenate([x.astype(BF16)] * RK_HEADS, axis=0) * hmn

    def stack_p(x):
        return jnp.concatenate([x.astype(BF16)] * RK_HEADS, axis=0) * hmp

    def stack_s(x):
        return jnp.concatenate([x.astype(BF16)] * RK_HEADS, axis=0) * bd

    def packed_t(x):
        full = (jnp.concatenate([x] * RK_HEADS, axis=0) * hmn_f).T
        out = full[0:HD, :]
        for h in range(1, RK_HEADS):
            out = out + full[h * HD:(h + 1) * HD, :]
        return out

    def mm(x, w, dims=NN):
        return _dot(x.astype(BF16), w, dims)

    chunks = range(T // C)
    rows = [slice(c * C, (c + 1) * C) for c in chunks]
    each = lambda fn, *lists: [fn(*args) for args in zip(*lists)]
    lwc = [lw[(c + 1) * C - 1:(c + 1) * C, :] for c in chunks]
    e_end = [jnp.exp(lwc[c] - lw[rows[c], :]) for c in chunks]
    kap_c = [kap[rw, :] for rw in rows]
    rh_c = [rh[rw, :] for rw in rows]
    bh4 = [stack_n(bh[rw, :]) for rw in rows]
    kh4 = [stack_n(kh[rw, :]) for rw in rows]
    v4 = [stack_n(v[rw, :]) for rw in rows]
    rcat = lambda *xs: jnp.concatenate(xs, axis=0)
    kap_rh = each(rcat, kap_c, rh_c)
    with_b = each(lambda x, w: mm(x, w, NT), kap_rh, bh4)
    with_k = each(lambda x, w: mm(x, w, NT), kap_rh, kh4)
    g_kb = [t[0:C, :] for t in with_b]
    a_rb = [t[C:2 * C, :] * msk_ref[M_LOWER] for t in with_b]
    a_k = [t[0:C, :] * msk_ref[M_STRICT] for t in with_k]
    a_rk = [t[C:2 * C, :] * msk_ref[M_LOWER] for t in with_k]
    x = [msk_ref[M_EYE] - gc * msk_ref[M_LEVEL0] for gc in g_kb]
    for lv in range(1, RK_LEVELS):
        xl = each(lambda xc, gc: mm(xc, stack_p(gc * msk_ref[M_LEVEL0 + lv])), x, g_kb)
        x = each(lambda xc, xlc: xc - mm(xlc, stack_p(xc)), x, xl)
    m1 = each(lambda xc, kc: mm(xc, stack_n(kc)), x, kap_c)
    bwt = [packed_t(beta[rows[c], :] * e_end[c]) for c in chunks]
    kwt = [packed_t(k[rows[c], :] * e_end[c]) for c in chunks]
    with_v = each(lambda ak, ark, kw, vc: mm(rcat(ak, ark, kw), vc), a_k, a_rk, kwt, v4)
    m2 = each(lambda xc, t: mm(xc, stack_n(t[0:C, :])), x, with_v)
    arb_bwt = each(rcat, a_rb, bwt)
    with_m1 = each(lambda l, mc: mm(l, stack_n(mc)), arb_bwt, m1)
    with_m2 = each(lambda l, mc: mm(l, stack_n(mc)), arb_bwt, m2)
    q1 = each(lambda rc, t: rc - t[0:C, :], rh_c, with_m1)
    q2 = each(lambda tv, t: tv[C:2 * C, :] - t[0:C, :], with_v, with_m2)
    pt = each(lambda lc, t: keye_ref[...] * jnp.exp(lc) - t[C:C + HD, :], lwc, with_m1)
    rt = each(lambda tv, t: tv[2 * C:2 * C + HD, :] - t[C:C + HD, :], with_v, with_m2)

    st = st_s[...]
    ys = []
    for c in chunks:
        with_st = mm(rcat(q1[c], pt[c]), stack_s(st))
        ys.append(with_st[0:C, :] + q2[c])
        st = with_st[C:C + HD, :] + rt[c]
    st_s[...] = st
    y = jnp.concatenate(ys, axis=0)

    inv_n = 1.0 / RK_HDIM
    mean = _sel_right(y, bd, terms=2) * inv_n
    d = y - mean
    var = _sel_right(d * d, bd, terms=2) * inv_n
    yn = d * lax.rsqrt(var + RK_GN_EPS) * lnw_ref[...] + lnb_ref[...]
    bonus = _sel_right(r * k * rkw_ref[...], bd, terms=2) * v
    y_ref[...] = ((yn + bonus) * g).astype(y_ref.dtype)


def _rwkv_params(w, l):
    row = lambda a: a.reshape(1, -1)
    n_dec, n_a = w["rwkv_w2"].shape[1], w["rwkv_a2"].shape[1]
    prm = {
        "mu": w["rwkv_mu"][l].reshape(4, RK_WIDTH),
        "w0": row(w["rwkv_w0"][l]), "a0": row(w["rwkv_a0"][l]),
        "w2": _pad_rows(w["rwkv_w2"][l], 0, RK_WIDTH).astype(BF16),
        "a2": _pad_rows(w["rwkv_a2"][l], n_dec, RK_WIDTH).astype(BF16),
        "g2": _pad_rows(w["rwkv_g2"][l], n_dec + n_a, RK_WIDTH).astype(BF16),
        "k_k": row(w["rwkv_k_k"][l]), "k_a": row(w["rwkv_k_a"][l]), "r_k": row(w["rwkv_r_k"][l]),
        "ln_w": row(w["rwkv_ln_w"][l]), "ln_b": row(w["rwkv_ln_b"][l]),
    }
    if l > 0:
        v1, v2 = w["rwkv_v1"][l - 1], w["rwkv_v2"][l - 1]
        prm["v0"] = row(w["rwkv_v0"][l - 1])
        prm["v1"] = jnp.pad(v1, ((0, 0), (0, LANES - v1.shape[1]))).astype(BF16)
        prm["v2"] = _pad_rows(v2, 0, LANES).astype(BF16)
    return prm


def _rwkv(z, prm, v_first):
    has_vres = v_first is not None
    C, T = RK_CHUNK, RK_TILE
    assert (1 << RK_LEVELS) == C and T % C == 0 and RK_HEADS * RK_HDIM == RK_WIDTH
    ccol = (HG_HEADS * 4 * HG_DIM + 3 * SW_WIDTH) // RK_WIDTH
    zcol = lambda off: pl.BlockSpec((T, RK_WIDTH), lambda i: (i, ccol + off))
    row = pl.BlockSpec((T, RK_WIDTH), lambda i: (i, 0))
    lane = np.arange(RK_WIDTH)
    bd = jnp.asarray((lane[:, None] // RK_HDIM) == (lane[None, :] // RK_HDIM), BF16)
    t = np.arange(RK_CUM)
    tri = jnp.asarray(((t[:, None] // C) == (t[None, :] // C)) & (t[None, :] <= t[:, None]), BF16)
    packed_masks, key_eye = _rwkv_masks()
    args = [z, z, z, z, prm["mu"], prm["w0"], prm["a0"], prm["w2"], prm["a2"], prm["g2"],
            prm["k_k"], prm["k_a"], prm["r_k"], prm["ln_w"], prm["ln_b"], bd, tri,
            jnp.asarray(packed_masks), jnp.asarray(key_eye),
            jnp.asarray(_head_mask(C, RK_HDIM), BF16), jnp.asarray(_head_mask(C, C), BF16)]
    in_specs = [zcol(0), zcol(1), zcol(2), zcol(3)] + [_vmem_spec()] * 17
    if has_vres:
        args += [v_first, prm["v0"], prm["v1"], prm["v2"]]
        in_specs += [row] + [_vmem_spec()] * 3
        out_specs = row
        out_shape = jax.ShapeDtypeStruct((SEQ, RK_WIDTH), BF16)
    else:
        out_specs = [row, row]
        out_shape = [jax.ShapeDtypeStruct((SEQ, RK_WIDTH), BF16), jax.ShapeDtypeStruct((SEQ, RK_WIDTH), F32)]
    out = pl.pallas_call(
        functools.partial(_rwkv_kernel, has_vres=has_vres),
        grid=(SEQ // T,),
        in_specs=in_specs,
        out_specs=out_specs,
        out_shape=out_shape,
        scratch_shapes=[pltpu.VMEM((RK_HDIM, RK_WIDTH), F32),
                        pltpu.VMEM((4, SUBLANES, RK_WIDTH), F32)],
        compiler_params=_params(("arbitrary",)),
        name="rwkv7",
    )(*args)
    if has_vres:
        return out, v_first
    return out[0], out[1]


def _post_kernel(h_ref, oa_ref, ob_ref, oc_ref, p_ref, wout_ref, gffn_ref, wup_ref, cw_ref, cb_ref,
                 wdn_ref, gple_ref, wg_ref, wp_ref, gfin_ref, out_ref,
                 hn_s, carry_s, act_s, *, final):
    T = ROW_TILE

    @pl.when(pl.program_id(0) == 0)
    def _():
        carry_s[...] = jnp.zeros_like(carry_s)

    mix = (_dot(oa_ref[...].astype(BF16), wout_ref[0:HG_WIDTH, :])
           + _dot(ob_ref[...].astype(BF16), wout_ref[HG_WIDTH:HG_WIDTH + SW_WIDTH, :])
           + _dot(oc_ref[...].astype(BF16), wout_ref[HG_WIDTH + SW_WIDTH:, :]))
    h1 = h_ref[...] + mix
    hn_s[...] = _rms(h1, gffn_ref[...]).astype(BF16)

    S8 = SUBLANES
    rid = lax.broadcasted_iota(jnp.int32, (S8, FFN_COLS), 0)

    def up(col):
        return _dot(hn_s[...], wup_ref[:, pl.ds(col, FFN_COLS)])

    def conv(col, u):
        cols = pl.ds(col, FFN_COLS)
        prev = carry_s[:, cols]
        carry_s[:, cols] = u[T - S8:T, :]
        r1 = pltpu.roll(u, 1, 0)
        r2 = pltpu.roll(u, 2, 0)
        last, last2 = prev[S8 - 1:S8, :], prev[S8 - 2:S8 - 1, :]
        top1 = jnp.where(rid == 0, last, r1[0:S8, :])
        top2 = jnp.where(rid == 0, last2, jnp.where(rid == 1, last, r2[0:S8, :]))
        u1 = jnp.concatenate([top1, r1[S8:, :]], axis=0)
        u2 = jnp.concatenate([top2, r2[S8:, :]], axis=0)
        return cb_ref[:, cols] + cw_ref[0:1, cols] * u2 + cw_ref[1:2, cols] * u1 + cw_ref[2:3, cols] * u

    def up_pair(j):
        return up(j * FFN_COLS), up(j * FFN_COLS + D_FF)

    n_steps = D_FF // FFN_COLS
    u_gate, u_val = up_pair(0)
    for j in range(n_steps):
        nxt = up_pair(j + 1) if j + 1 < n_steps else None
        gate = conv(j * FFN_COLS, u_gate)
        val = conv(j * FFN_COLS + D_FF, u_val)
        act_s[:, pl.ds(j * FFN_COLS, FFN_COLS)] = (_silu(gate) * val).astype(BF16)
        if nxt is not None:
            u_gate, u_val = nxt

    h2 = h1 + _dot(act_s[...], wdn_ref[...])
    gate = _sigmoid(_dot(_rms(h2, gple_ref[...]).astype(BF16), wg_ref[...]))
    pe = _dot(p_ref[...].astype(BF16), wp_ref[...])
    h3 = h2 + gate * pe
    out_ref[...] = _rms(h3, gfin_ref[...]) if final else h3


def _post(h, o_a, o_b, o_c, p_l, prm, final):
    T = ROW_TILE
    tile = lambda w: pl.BlockSpec((T, w), lambda i: (i, 0))
    return pl.pallas_call(
        functools.partial(_post_kernel, final=final),
        grid=(SEQ // T,),
        in_specs=[tile(D_MODEL), tile(HG_WIDTH), tile(SW_WIDTH), tile(RK_WIDTH), tile(PLE_DIM)]
                 + [_vmem_spec()] * 10,
        out_specs=tile(D_MODEL),
        out_shape=jax.ShapeDtypeStruct((SEQ, D_MODEL), F32),
        scratch_shapes=[pltpu.VMEM((T, D_MODEL), BF16),
                        pltpu.VMEM((SUBLANES, 2 * D_FF), F32), pltpu.VMEM((T, D_FF), BF16)],
        compiler_params=_params(("arbitrary",)),
        name="outproj_ffn_ple",
    )(h, o_a, o_b, o_c, p_l, prm["w_out"], prm["g_ffn"], prm["w_up"], prm["conv_w"], prm["conv_b"],
      prm["w_down"], prm["g_ple"], prm["w_gate"], prm["w_ple"], prm["g_final"])


def _pad_rows(w, start, total):
    out = jnp.zeros((total, w.shape[1]), w.dtype)
    return lax.dynamic_update_slice(out, w, (start, 0))


def kernel(x, p, w_in, w_out, norm_mix_g, norm_ffn_g, norm_ple_g, final_norm_g, hgrn_lower_bounds, hgrn_gnorm_g, rwkv_mu, rwkv_w0, rwkv_w2, rwkv_a0, rwkv_a2, rwkv_g2, rwkv_k_k, rwkv_k_a, rwkv_r_k, rwkv_ln_w, rwkv_ln_b, rwkv_v0, rwkv_v1, rwkv_v2, ffn_up, ffn_conv_w, ffn_conv_b, ffn_down, ple_proj, ple_gate):
    row = lambda a: a.reshape(1, -1)
    h = x.reshape(SEQ, D_MODEL)
    v_first = None
    rwkv_w = dict(rwkv_mu=rwkv_mu, rwkv_w0=rwkv_w0, rwkv_w2=rwkv_w2, rwkv_a0=rwkv_a0, rwkv_a2=rwkv_a2,
                  rwkv_g2=rwkv_g2, rwkv_k_k=rwkv_k_k, rwkv_k_a=rwkv_k_a, rwkv_r_k=rwkv_r_k,
                  rwkv_ln_w=rwkv_ln_w, rwkv_ln_b=rwkv_ln_b, rwkv_v0=rwkv_v0, rwkv_v1=rwkv_v1, rwkv_v2=rwkv_v2)
    for l in range(DEPTH):
        z = _inproj(h, row(norm_mix_g[l]), w_in[l].astype(BF16))
        o_a = _hgrn(z, hgrn_lower_bounds, row(hgrn_gnorm_g[l]), l)
        o_b = _dilated_mix(z)
        o_c, v_first = _rwkv(z, _rwkv_params(rwkv_w, l), v_first)
        post = {
            "w_out": w_out[l].astype(BF16), "g_ffn": row(norm_ffn_g[l]),
            "w_up": ffn_up[l].astype(BF16), "conv_w": ffn_conv_w[l], "conv_b": row(ffn_conv_b[l]),
            "w_down": ffn_down[l].astype(BF16), "g_ple": row(norm_ple_g[l]),
            "w_gate": ple_gate[l].astype(BF16), "w_ple": ple_proj[l].astype(BF16),
            "g_final": row(final_norm_g),
        }
        h = _post(h, o_a, o_b, o_c, p[l].reshape(SEQ, PLE_DIM), post, final=l == DEPTH - 1)
    return h.reshape(1, SEQ, D_MODEL)
```
